```python
import math, functools
import jax, jax.numpy as jnp
from jax import lax
import numpy as np

D_MODEL = 1024
BATCH = 1
SEQ = 16384
DEPTH = 1
DEC_BATCH = 128
DEC_SEQ = 1
PAST_LEN = 16384
PAGE_SIZE = 128

GDN_HEADS = 8
GDN_DK = 64
GDN_DV = 64
GDN_CONV = 4
GDN_CHUNK = 64
MLA_HEADS = 8
MLA_Q_LORA = 384
MLA_KV_LORA = 256
MLA_NOPE = 64
MLA_ROPE = 32
MLA_V = 64
ROPE_THETA = 10000.0
Q_BLOCK = 128
MEM_TOKENS = 256
MEM_HEADS = 4
MEM_HEAD_DIM = 128
N_EXPERTS = 32
TOP_K = 4
D_FF = 1024
SWIGLU_LIMIT = 7.0
SWIGLU_ALPHA = 1.702
NORM_EPS = 1e-6

GDN_QK_W = GDN_HEADS * GDN_DK
GDN_V_W = GDN_HEADS * GDN_DV
GDN_CONV_W = 2 * GDN_QK_W + GDN_V_W
MLA_KV_A = MLA_KV_LORA + MLA_ROPE
MLA_QK_HEAD = MLA_NOPE + MLA_ROPE
MLA_SCALE = MLA_QK_HEAD ** -0.5
MLA_V_W = MLA_HEADS * MLA_V
MEM_W = MEM_HEADS * MEM_HEAD_DIM
IN_SIZES = (GDN_CONV_W, GDN_V_W, GDN_HEADS, GDN_HEADS, MLA_Q_LORA, MLA_KV_A, D_MODEL, D_MODEL)
IN_WIDTH = sum(IN_SIZES)

kernel_name = 'hybrid_gdn_mla_memx_moe_step'


def rmsnorm(x, g):
    xf = x.astype(jnp.float32)
    y = xf * lax.rsqrt(jnp.mean(xf * xf, axis=-1, keepdims=True) + NORM_EPS)
    return (y * g.astype(jnp.float32)).astype(x.dtype)


def l2norm(x):
    xf = x.astype(jnp.float32)
    return xf * lax.rsqrt(jnp.sum(xf * xf, axis=-1, keepdims=True) + NORM_EPS)


def rope_cos_sin(pos):
    half = MLA_ROPE // 2
    inv = 1.0 / (ROPE_THETA ** (jnp.arange(half, dtype=jnp.float32) / half))
    ang = pos.astype(jnp.float32)[:, None] * inv[None, :]
    return jnp.cos(ang), jnp.sin(ang)


def apply_rope(x, cos, sin):
    xf = x.astype(jnp.float32)
    x1, x2 = jnp.split(xf, 2, axis=-1)
    return jnp.concatenate([x1 * cos - x2 * sin, x2 * cos + x1 * sin], axis=-1).astype(x.dtype)


def split_in(z):
    offs = np.cumsum(IN_SIZES)[:-1].tolist()
    return jnp.split(z, offs, axis=-1)


def causal_conv_silu(x, buf, w):
    L = x.shape[1]
    xp = jnp.concatenate([buf.astype(x.dtype), x], axis=1)
    y = xp[:, 0:L] * w[0]
    for j in range(1, GDN_CONV):
        y = y + xp[:, j:j + L] * w[j]
    return jax.nn.silu(y), xp[:, xp.shape[1] - (GDN_CONV - 1):]


def gated_delta_chunked(q, k, v, g, beta, s0):
    Bsz, L, H, DK = q.shape
    DV = v.shape[-1]
    C = GDN_CHUNK if L >= GDN_CHUNK else L
    pad = (-L) % C
    if pad:
        pw = ((0, 0), (0, pad), (0, 0), (0, 0))
        q, k, v = jnp.pad(q, pw), jnp.pad(k, pw), jnp.pad(v, pw)
        g = jnp.pad(g, pw[:3])
        beta = jnp.pad(beta, pw[:3])
    n = (L + pad) // C

    def chunks(t):
        t = t.reshape((Bsz, n, C, H) + t.shape[3:])
        return jnp.moveaxis(t, (1, 3), (0, 2))

    qc, kc, vc, bc = chunks(q), chunks(k), chunks(v), chunks(beta)
    gc = jnp.cumsum(chunks(g), axis=-1)
    idx = jnp.arange(C)
    incl = idx[:, None] >= idx[None, :]
    strict = idx[:, None] > idx[None, :]
    decay = jnp.exp(jnp.where(incl, gc[..., :, None] - gc[..., None, :], -jnp.inf))
    kb = kc * bc[..., None]
    A = jnp.where(strict, jnp.einsum('nbhid,nbhjd->nbhij', kb, kc) * decay, 0.0)
    eye = jnp.eye(C, dtype=jnp.float32)
    T = lax.linalg.triangular_solve(eye + A, jnp.broadcast_to(eye, A.shape), left_side=True,
                                    lower=True, unit_diagonal=True)
    u = jnp.einsum('nbhij,nbhjd->nbhid', T, vc * bc[..., None])
    w = jnp.einsum('nbhij,nbhjd->nbhid', T, kb * jnp.exp(gc)[..., None])
    qk = jnp.where(incl, jnp.einsum('nbhid,nbhjd->nbhij', qc, kc) * decay, 0.0)
    qg = qc * jnp.exp(gc)[..., None]
    kg = kc * jnp.exp(gc[..., -1:] - gc)[..., None]
    g_last = jnp.exp(gc[..., -1])

    def step(S, xs):
        qg_c, kg_c, u_c, w_c, qk_c, gl_c = xs
        v_new = u_c - jnp.einsum('bhcd,bhde->bhce', w_c, S)
        o = jnp.einsum('bhcd,bhde->bhce', qg_c, S) + jnp.einsum('bhij,bhje->bhie', qk_c, v_new)
        S = S * gl_c[..., None, None] + jnp.einsum('bhcd,bhce->bhde', kg_c, v_new)
        return S, o

    S, o = lax.scan(step, s0, (qg, kg, u, w, qk, g_last))
    o = jnp.moveaxis(o, (0, 2), (1, 3)).reshape(Bsz, n * C, H, DV)[:, :L]
    return o, S


def gdn_branch(qkv_raw, z, a, b, conv_buf, s0, conv_w, a_log, dt_bias, norm_g):
    Bsz, L, _ = qkv_raw.shape
    qkv, new_buf = causal_conv_silu(qkv_raw, conv_buf, conv_w)
    q, k, v = jnp.split(qkv, [GDN_QK_W, 2 * GDN_QK_W], axis=-1)
    q = l2norm(q.reshape(Bsz, L, GDN_HEADS, GDN_DK)) * (GDN_DK ** -0.5)
    k = l2norm(k.reshape(Bsz, L, GDN_HEADS, GDN_DK))
    v = v.reshape(Bsz, L, GDN_HEADS, GDN_DV).astype(jnp.float32)
    g = -jnp.exp(a_log.astype(jnp.float32)) * jax.nn.softplus(a.astype(jnp.float32) + dt_bias.astype(jnp.float32))
    beta = jax.nn.sigmoid(b.astype(jnp.float32))
    o, S = gated_delta_chunked(q, k, v, g, beta, s0.astype(jnp.float32))
    o = rmsnorm(o, norm_g) * jax.nn.silu(z.reshape(Bsz, L, GDN_HEADS, GDN_DV).astype(jnp.float32))
    return o.reshape(Bsz, L, GDN_V_W).astype(qkv_raw.dtype), new_buf, S.astype(s0.dtype)


def mla_project(q_a, kv_a, cos, sin, q_norm, w_qb, kv_norm):
    Bsz, L, _ = q_a.shape
    q = (rmsnorm(q_a, q_norm) @ w_qb).reshape(Bsz, L, MLA_HEADS, MLA_QK_HEAD)
    q_nope = q[..., :MLA_NOPE]
    q_pe = apply_rope(q[..., MLA_NOPE:], cos[:, None, :], sin[:, None, :])
    c_kv = rmsnorm(kv_a[..., :MLA_KV_LORA], kv_norm)
    k_pe = apply_rope(kv_a[..., MLA_KV_LORA:], cos, sin)
    return q_nope, q_pe, c_kv, k_pe


def mla_attend_prompt(q_nope, q_pe, c_kv, k_pe, w_kvb):
    Bsz, L, H, _ = q_nope.shape
    kv = jnp.einsum('blc,chd->blhd', c_kv, w_kvb.reshape(MLA_KV_LORA, H, MLA_NOPE + MLA_V))
    k_nope, v = kv[..., :MLA_NOPE], kv[..., MLA_NOPE:]
    qb = min(Q_BLOCK, L)
    kpos = jnp.arange(L)

    def block(i):
        start = i * qb
        qn = lax.dynamic_slice_in_dim(q_nope, start, qb, axis=1)
        qp = lax.dynamic_slice_in_dim(q_pe, start, qb, axis=1)
        s = (jnp.einsum('bqhd,bkhd->bhqk', qn, k_nope)
             + jnp.einsum('bqhr,bkr->bhqk', qp, k_pe)).astype(jnp.float32) * MLA_SCALE
        qpos = start + jnp.arange(qb)
        s = jnp.where(kpos[None, :] <= qpos[:, None], s, -jnp.inf)
        p = jax.nn.softmax(s, axis=-1).astype(v.dtype)
        return jnp.einsum('bhqk,bkhd->bqhd', p, v)

    o = lax.map(block, jnp.arange(L // qb))
    return jnp.moveaxis(o, 0, 1).reshape(Bsz, L, H * MLA_V)


def mla_attend_paged(q_nope, q_pe, c_kv, k_pe, w_kvb, pool_lat, pool_rope, layer, page_table):
    Bsz, T, H, _ = q_nope.shape
    w = w_kvb.reshape(MLA_KV_LORA, H, MLA_NOPE + MLA_V)
    w_uk, w_uv = w[..., :MLA_NOPE], w[..., MLA_NOPE:]
    q_lat = jnp.einsum('bthd,chd->bhtc', q_nope, w_uk)
    q_r = jnp.swapaxes(q_pe, 1, 2)

    def scores(lat, rope):
        return (jnp.einsum('bhtc,bpc->bhtp', q_lat, lat)
                + jnp.einsum('bhtr,bpr->bhtp', q_r, rope)).astype(jnp.float32) * MLA_SCALE

    def merge(carry, s, lat):
        m, l, acc = carry
        m_new = jnp.maximum(m, s.max(-1))
        corr = jnp.exp(m - m_new)
        p = jnp.exp(s - m_new[..., None])
        l = l * corr + p.sum(-1)
        acc = acc * corr[..., None] + jnp.einsum('bhtp,bpc->bhtc', p, lat.astype(jnp.float32))
        return (m_new, l, acc)

    def page_step(carry, phys):
        lat = pool_lat[layer, phys]
        rope = pool_rope[layer, phys]
        return merge(carry, scores(lat, rope), lat), None

    init = (jnp.full((Bsz, H, T), -jnp.inf, jnp.float32),
            jnp.zeros((Bsz, H, T), jnp.float32),
            jnp.zeros((Bsz, H, T, MLA_KV_LORA), jnp.float32))
    carry, _ = lax.scan(page_step, init, page_table.T)
    tpos = jnp.arange(T)
    s_new = jnp.where(tpos[:, None] >= tpos[None, :], scores(c_kv, k_pe), -jnp.inf)
    m, l, acc = merge(carry, s_new, c_kv)
    o_lat = (acc / l[..., None]).astype(q_nope.dtype)
    o = jnp.einsum('bhtc,chd->bthd', o_lat, w_uv)
    return o.reshape(Bsz, T, H * MLA_V)


def mem_kv(mem, g, w_mk, w_mv):
    m = rmsnorm(mem, g)
    Bsz, M, _ = mem.shape
    return ((m @ w_mk).reshape(Bsz, M, MEM_HEADS, MEM_HEAD_DIM),
            (m @ w_mv).reshape(Bsz, M, MEM_HEADS, MEM_HEAD_DIM))


def mem_attend(h, mk, mv, w_mq, w_mo):
    Bsz, L, _ = h.shape
    q = (h @ w_mq).reshape(Bsz, L, MEM_HEADS, MEM_HEAD_DIM)
    s = jnp.einsum('blhd,bmhd->bhlm', q, mk).astype(jnp.float32) * (MEM_HEAD_DIM ** -0.5)
    p = jax.nn.softmax(s, axis=-1).astype(h.dtype)
    o = jnp.einsum('bhlm,bmhd->blhd', p, mv).reshape(Bsz, L, MEM_W)
    return o @ w_mo


def moe(h, router_w, router_b, w_gate, b_gate, w_up, b_up, w_down, b_down):
    Bsz, L, D = h.shape
    t = h.reshape(Bsz * L, D)
    logits = (t @ router_w + router_b).astype(jnp.float32)
    top_v, top_i = lax.top_k(logits, TOP_K)
    wts = jax.nn.softmax(top_v, axis=-1)
    comb = jnp.einsum('nk,nke->ne', wts, jax.nn.one_hot(top_i, N_EXPERTS, dtype=jnp.float32)).astype(h.dtype)
    y = jnp.zeros_like(t)
    for e in range(N_EXPERTS):
        gate = jnp.minimum(t @ w_gate[e] + b_gate[e], SWIGLU_LIMIT)
        up = jnp.clip(t @ w_up[e] + b_up[e], -SWIGLU_LIMIT, SWIGLU_LIMIT)
        act = (up + 1.0) * gate * jax.nn.sigmoid(SWIGLU_ALPHA * gate)
        y = y + comb[:, e:e + 1] * (act @ w_down[e] + b_down[e])
    return y.reshape(Bsz, L, D)


def setup_inputs(seed: int = 0) -> dict:
    key = jax.random.key(seed)
    keys = iter(jax.random.split(key, 64))
    f32 = jnp.float32

    def nrm(shape, scale=1.0):
        return jax.random.normal(next(keys), shape, f32) * scale

    def gain(shape):
        return 1.0 + 0.1 * jax.random.normal(next(keys), shape, f32)

    L = DEPTH
    n_pages = PAST_LEN // PAGE_SIZE
    n_used = DEC_BATCH * n_pages
    n_phys = n_used + max(1, n_used // 4)
    page_table = jax.random.permutation(next(keys), n_phys)[:n_used].reshape(DEC_BATCH, n_pages).astype(jnp.int32)
    a_log = jnp.log(jax.random.uniform(next(keys), (L, GDN_HEADS), f32, 1.0, 16.0))
    dt = jnp.exp(jax.random.uniform(next(keys), (L, GDN_HEADS), f32, math.log(1e-3), math.log(1e-1)))
    dt_bias = jnp.log(jnp.expm1(dt))
    return {
        'x_prompt': nrm((BATCH, SEQ, D_MODEL)),
        'x_sample': nrm((DEC_BATCH, DEC_SEQ, D_MODEL)),
        'mem_prompt': nrm((BATCH, MEM_TOKENS, D_MODEL)),
        'cache_kv_latent': nrm((L, n_phys, PAGE_SIZE, MLA_KV_LORA)),
        'cache_k_rope': nrm((L, n_phys, PAGE_SIZE, MLA_ROPE)),
        'cache_mem_k': nrm((L, DEC_BATCH, MEM_TOKENS, MEM_HEADS, MEM_HEAD_DIM)),
        'cache_mem_v': nrm((L, DEC_BATCH, MEM_TOKENS, MEM_HEADS, MEM_HEAD_DIM)),
        'state_gdn_conv': nrm((L, DEC_BATCH, GDN_CONV - 1, GDN_CONV_W)),
        'state_gdn_ssm': nrm((L, DEC_BATCH, GDN_HEADS, GDN_DK, GDN_DV), 0.2),
        'page_table': page_table,
        'norm_mix': gain((L, D_MODEL)),
        'w_in': nrm((L, D_MODEL, IN_WIDTH), D_MODEL ** -0.5),
        'gdn_conv_w': nrm((L, GDN_CONV, GDN_CONV_W), GDN_CONV ** -0.5),
        'gdn_a_log': a_log,
        'gdn_dt_bias': dt_bias,
        'gdn_norm': gain((L, GDN_DV)),
        'gdn_w_up': nrm((L, GDN_V_W, D_MODEL), GDN_V_W ** -0.5),
        'mla_q_norm': gain((L, MLA_Q_LORA)),
        'mla_w_qb': nrm((L, MLA_Q_LORA, MLA_HEADS * MLA_QK_HEAD), MLA_Q_LORA ** -0.5),
        'mla_kv_norm': gain((L, MLA_KV_LORA)),
        'mla_w_kvb': nrm((L, MLA_KV_LORA, MLA_HEADS * (MLA_NOPE + MLA_V)), MLA_KV_LORA ** -0.5),
        'mla_w_up': nrm((L, MLA_V_W, D_MODEL), MLA_V_W ** -0.5),
        'w_out': nrm((L, D_MODEL, D_MODEL), D_MODEL ** -0.5),
        'norm_mem': gain((L, D_MODEL)),
        'mem_in_norm': gain((L, D_MODEL)),
        'w_mq': nrm((L, D_MODEL, MEM_W), D_MODEL ** -0.5),
        'w_mk': nrm((L, D_MODEL, MEM_W), D_MODEL ** -0.5),
        'w_mv': nrm((L, D_MODEL, MEM_W), D_MODEL ** -0.5),
        'w_mo': nrm((L, MEM_W, D_MODEL), MEM_W ** -0.5),
        'norm_ffn': gain((L, D_MODEL)),
        'router_w': nrm((L, D_MODEL, N_EXPERTS), D_MODEL ** -0.5),
        'router_b': nrm((L, N_EXPERTS), 0.01),
        'w_gate': nrm((L, N_EXPERTS, D_MODEL, D_FF), D_MODEL ** -0.5),
        'b_gate': nrm((L, N_EXPERTS, D_FF), 0.01),
        'w_up': nrm((L, N_EXPERTS, D_MODEL, D_FF), D_MODEL ** -0.5),
        'b_up': nrm((L, N_EXPERTS, D_FF), 0.01),
        'w_down': nrm((L, N_EXPERTS, D_FF, D_MODEL), D_FF ** -0.5),
        'b_down': nrm((L, N_EXPERTS, D_MODEL), 0.01),
        'norm_final': gain((D_MODEL,)),
    }


def reference(x_prompt, x_sample, mem_prompt, cache_kv_latent, cache_k_rope, cache_mem_k, cache_mem_v,
              state_gdn_conv, state_gdn_ssm, page_table,
              norm_mix, w_in, gdn_conv_w, gdn_a_log, gdn_dt_bias, gdn_norm, gdn_w_up,
              mla_q_norm, mla_w_qb, mla_kv_norm, mla_w_kvb, mla_w_up, w_out,
              norm_mem, mem_in_norm, w_mq, w_mk, w_mv, w_mo,
              norm_ffn, router_w, router_b, w_gate, b_gate, w_up, b_up, w_down, b_down, norm_final):

    def layer(l, x, pos, conv_buf, ssm0, mk, mv, attend):
        h = rmsnorm(x, norm_mix[l])
        qkv_raw, z_gdn, a_gdn, b_gdn, q_a, kv_a, gate_a, gate_b = split_in(h @ w_in[l])
        o_gdn, conv_new, ssm_new = gdn_branch(qkv_raw, z_gdn, a_gdn, b_gdn, conv_buf, ssm0, gdn_conv_w[l],
                                              gdn_a_log[l], gdn_dt_bias[l], gdn_norm[l])
        cos, sin = rope_cos_sin(pos)
        q_nope, q_pe, c_kv, k_pe = mla_project(q_a, kv_a, cos, sin, mla_q_norm[l], mla_w_qb[l], mla_kv_norm[l])
        o_mla = attend(q_nope, q_pe, c_kv, k_pe, mla_w_kvb[l])
        merged = (jax.nn.sigmoid(gate_a) * (o_gdn @ gdn_w_up[l])
                  + jax.nn.sigmoid(gate_b) * (o_mla @ mla_w_up[l]))
        x = x + merged @ w_out[l]
        x = x + mem_attend(rmsnorm(x, norm_mem[l]), mk, mv, w_mq[l], w_mo[l])
        x = x + moe(rmsnorm(x, norm_ffn[l]), router_w[l], router_b[l], w_gate[l], b_gate[l],
                    w_up[l], b_up[l], w_down[l], b_down[l])
        return x, c_kv, k_pe, conv_new, ssm_new

    Bp, Lp, _ = x_prompt.shape
    pos_p = jnp.arange(Lp)
    past_len = page_table.shape[1] * PAGE_SIZE
    pos_s = past_len + jnp.arange(x_sample.shape[1])
    conv0 = jnp.zeros((Bp, GDN_CONV - 1, GDN_CONV_W), x_prompt.dtype)
    ssm0 = jnp.zeros((Bp, GDN_HEADS, GDN_DK, GDN_DV), x_prompt.dtype)

    xp, xs = x_prompt, x_sample
    lat_p, rope_p, conv_p, ssm_p, memk_p, memv_p = [], [], [], [], [], []
    lat_s, rope_s, conv_s, ssm_s = [], [], [], []
    for l in range(DEPTH):
        mk_p, mv_p = mem_kv(mem_prompt, mem_in_norm[l], w_mk[l], w_mv[l])
        xp, c1, r1, cv1, s1 = layer(l, xp, pos_p, conv0, ssm0, mk_p, mv_p, mla_attend_prompt)
        sample_attend = functools.partial(mla_attend_paged, pool_lat=cache_kv_latent, pool_rope=cache_k_rope,
                                          layer=l, page_table=page_table)
        xs, c2, r2, cv2, s2 = layer(l, xs, pos_s, state_gdn_conv[l], state_gdn_ssm[l],
                                    cache_mem_k[l], cache_mem_v[l], sample_attend)
        lat_p.append(c1); rope_p.append(r1); conv_p.append(cv1); ssm_p.append(s1)
        memk_p.append(mk_p); memv_p.append(mv_p)
        lat_s.append(c2); rope_s.append(r2); conv_s.append(cv2); ssm_s.append(s2)

    y_prompt = rmsnorm(xp, norm_final)
    y_sample = rmsnorm(xs, norm_final)
    return (y_prompt, y_sample,
            jnp.stack(lat_p), jnp.stack(rope_p), jnp.stack(conv_p), jnp.stack(ssm_p),
            jnp.stack(memk_p), jnp.stack(memv_p),
            jnp.stack(lat_s), jnp.stack(rope_s), jnp.stack(conv_s), jnp.stack(ssm_s))
```

```python
import functools
import math

import numpy as np
import jax
import jax.numpy as jnp
from jax import lax
from jax.experimental import pallas as pl
from jax.experimental.pallas import tpu as pltpu

F32 = jnp.float32
BF16 = jnp.bfloat16
I32 = jnp.int32

D_MODEL = 1024
PAGE_SIZE = 128
GDN_HEADS = 8
GDN_DK = 64
GDN_DV = 64
GDN_CONV = 4
GDN_CHUNK = 64
MLA_HEADS = 8
MLA_Q_LORA = 384
MLA_KV_LORA = 256
MLA_NOPE = 64
MLA_ROPE = 32
MLA_V = 64
ROPE_THETA = 10000.0
MEM_TOKENS = 256
MEM_HEADS = 4
MEM_HEAD_DIM = 128
N_EXPERTS = 32
TOP_K = 4
D_FF = 1024
SWIGLU_LIMIT = 7.0
SWIGLU_ALPHA = 1.702
NORM_EPS = 1e-6

GDN_QK_W = GDN_HEADS * GDN_DK
GDN_V_W = GDN_HEADS * GDN_DV
GDN_CONV_W = 2 * GDN_QK_W + GDN_V_W
MLA_QK_HEAD = MLA_NOPE + MLA_ROPE
MLA_SCALE = MLA_QK_HEAD ** -0.5
MEM_W = MEM_HEADS * MEM_HEAD_DIM
IN_SIZES = (GDN_CONV_W, GDN_V_W, GDN_HEADS, GDN_HEADS, MLA_Q_LORA,
            MLA_KV_LORA + MLA_ROPE, D_MODEL, D_MODEL)

LANE = 128
LOG2E = 1.4426950408889634
VMEM_LIMIT = 56 * 1024 * 1024

SEG_QKV = (0, 1536)
SEG_Z = (1536, 2048)
SEG_AB = (2048, 2176)
SEG_QA = (2176, 2560)
SEG_LAT = (2560, 2816)
SEG_ROPE = (2816, 2944)
SEG_GA = (2944, 3968)
SEG_GB = (3968, 4992)
PACKED_W = 4992
ROPE_LANE0 = 64


def _cparams(sem):
    return pltpu.CompilerParams(dimension_semantics=sem, vmem_limit_bytes=VMEM_LIMIT)


def _rms(x, g):
    return x * lax.rsqrt(jnp.mean(x * x, axis=-1, keepdims=True) + NORM_EPS) * g


def _dot(a, b):
    return jnp.dot(a, b, preferred_element_type=F32)


def _dot_nt(a, b):
    return lax.dot_general(a, b, (((1,), (1,)), ((), ())), preferred_element_type=F32)


def _dot_tn(a, b):
    return lax.dot_general(a, b, (((0,), (0,)), ((), ())), preferred_element_type=F32)


def _split3(x):
    x1 = x.astype(BF16)
    r = x - x1.astype(F32)
    x2 = r.astype(BF16)
    x3 = (r - x2.astype(F32)).astype(BF16)
    return x1, x2, x3


def _dot_lhs_exact(x, m):
    x1, x2, x3 = _split3(x)
    return _dot(x1, m) + _dot(x2, m) + _dot(x3, m)


def _dot_rhs_exact(m, x):
    x1, x2, x3 = _split3(x)
    return _dot(m, x1) + _dot(m, x2) + _dot(m, x3)


def _dot_f32(a, b):
    return jnp.dot(a, b, precision=lax.Precision.HIGHEST, preferred_element_type=F32)


def _inproj_kernel(x_ref, g_ref, w_ref, qkv_ref, z_ref, ab_ref, qa_ref, lat_ref, rope_ref,
                   sga_ref, sgb_ref, *, precise):
    h = _rms(x_ref[...], g_ref[...])
    if not precise:
        h = h.astype(BF16)

    def seg(s):
        w = w_ref[:, s[0]:s[1]]
        return _dot_f32(h, w) if precise else _dot(h, w)

    qkv_ref[...] = seg(SEG_QKV)
    z_ref[...] = seg(SEG_Z)
    ab_ref[...] = seg(SEG_AB)
    qa_ref[...] = seg(SEG_QA)
    lat_ref[...] = seg(SEG_LAT)
    rope_ref[...] = seg(SEG_ROPE)
    sga_ref[...] = jax.nn.sigmoid(seg(SEG_GA)).astype(sga_ref.dtype)
    sgb_ref[...] = jax.nn.sigmoid(seg(SEG_GB)).astype(sgb_ref.dtype)


def _pack_w_in(w):
    offs = np.cumsum(IN_SIZES)[:-1].tolist()
    qkv, z, a, b, qa, kva, ga, gb = jnp.split(w, offs, axis=1)
    k = w.shape[0]
    ab = jnp.concatenate([a, b, jnp.zeros((k, LANE - 2 * GDN_HEADS), w.dtype)], axis=1)
    rope = jnp.concatenate([jnp.zeros((k, ROPE_LANE0), w.dtype), kva[:, MLA_KV_LORA:],
                            jnp.zeros((k, LANE - ROPE_LANE0 - MLA_ROPE), w.dtype)], axis=1)
    return jnp.concatenate([qkv, z, ab, qa, kva[:, :MLA_KV_LORA], rope, ga, gb], axis=1)


def _inproj(x, g, wp, tm, precise=False):
    m = x.shape[0]
    widths = [s[1] - s[0] for s in (SEG_QKV, SEG_Z, SEG_AB, SEG_QA, SEG_LAT, SEG_ROPE, SEG_GA, SEG_GB)]
    dts = [F32] * 6 + ([F32] * 2 if precise else [BF16] * 2)
    row = lambda i: (i, 0)
    const = lambda i: (0, 0)
    return pl.pallas_call(
        functools.partial(_inproj_kernel, precise=precise),
        grid=(m // tm,),
        in_specs=[pl.BlockSpec((tm, D_MODEL), row), pl.BlockSpec((1, D_MODEL), const),
                  pl.BlockSpec((D_MODEL, PACKED_W), const)],
        out_specs=[pl.BlockSpec((tm, w), row) for w in widths],
        out_shape=[jax.ShapeDtypeStruct((m, w), dt) for w, dt in zip(widths, dts)],
        compiler_params=_cparams(("arbitrary",)),
        name="inproj_sample" if precise else "inproj",
    )(x, g, wp)


GDN_TB = 256
GDN_NC = GDN_TB // GDN_CHUNK
GDN_PAIRS = GDN_HEADS // 2
TAIL = 8


def _gdn_tables(tb):
    c = GDN_CHUNK
    bd = np.kron(np.eye(GDN_HEADS), np.ones((GDN_DK, GDN_DK)))
    r = np.arange(tb)
    lblk = ((r[:, None] >= r[None, :]) & (r[:, None] // c == r[None, :] // c)).astype(np.float32)
    eg = np.zeros((LANE, GDN_QK_W), np.float32)
    eb = np.zeros((LANE, GDN_QK_W), np.float32)
    for h in range(GDN_HEADS):
        eg[h, h * GDN_DK:(h + 1) * GDN_DK] = 1.0
        eb[GDN_HEADS + h, h * GDN_DK:(h + 1) * GDN_DK] = 1.0
    return (jnp.asarray(bd, BF16), jnp.asarray(lblk, BF16), jnp.asarray(eg, BF16), jnp.asarray(eb, BF16))


def _softplus(x):
    return jnp.maximum(x, 0.0) + jnp.log1p(jnp.exp(-jnp.abs(x)))


def _gdn_kernel(qkv_ref, z_ref, ab_ref, cw_ref, alog_ref, dtb_ref, gn_ref, bd_ref, lblk_ref, eg_ref, eb_ref,
                o_ref, conv_ref, ssm_ref, xp_ref, s_ref):
    tb = GDN_TB
    c = GDN_CHUNK
    i = pl.program_id(0)

    @pl.when(i == 0)
    def _():
        xp_ref[0:TAIL, :] = jnp.zeros((TAIL, GDN_CONV_W), F32)
        s_ref[...] = jnp.zeros_like(s_ref)

    xp_ref[TAIL:TAIL + tb, :] = qkv_ref[...]
    y = cw_ref[0:1, :] * xp_ref[TAIL - 3:TAIL - 3 + tb, :]
    for j in range(1, GDN_CONV):
        y = y + cw_ref[j:j + 1, :] * xp_ref[TAIL - 3 + j:TAIL - 3 + j + tb, :]
    conv_ref[...] = xp_ref[TAIL + tb - (GDN_CONV - 1):TAIL + tb, :]
    xp_ref[0:TAIL, :] = xp_ref[tb:tb + TAIL, :]

    act = y * jax.nn.sigmoid(y)
    q = act[:, 0:GDN_QK_W]
    k = act[:, GDN_QK_W:2 * GDN_QK_W]
    v = act[:, 2 * GDN_QK_W:]
    bd = bd_ref[...]

    def segsum(t):
        t1 = t.astype(BF16)
        t2 = (t - t1.astype(F32)).astype(BF16)
        return _dot(t1, bd) + _dot(t2, bd)

    qn = q * lax.rsqrt(segsum(q * q) + NORM_EPS) * (GDN_DK ** -0.5)
    kn = k * lax.rsqrt(segsum(k * k) + NORM_EPS)

    ab = ab_ref[...]
    g128 = -jnp.exp(alog_ref[...]) * _softplus(ab + dtb_ref[...])
    beta128 = jax.nn.sigmoid(ab)
    gcum = _dot_rhs_exact(lblk_ref[...], g128)
    gc_e = _dot_lhs_exact(gcum, eg_ref[...])
    beta_e = _dot_lhs_exact(beta128, eb_ref[...])
    eg_e = jnp.exp(gc_e)
    kb = kn * beta_e
    vb = v * beta_e
    kbeg = kb * eg_e
    qg = qn * eg_e

    lane = lax.broadcasted_iota(I32, (c, LANE), 1)
    sub = lax.broadcasted_iota(I32, (c, LANE), 0)
    jj = jnp.where(lane >= GDN_DK, lane - GDN_DK, lane)
    m_incl = sub >= jj
    m_strict = sub > jj
    m_diag = sub == jj
    lane2 = lax.broadcasted_iota(I32, (2 * c, LANE), 1)
    sub2 = lax.broadcasted_iota(I32, (2 * c, LANE), 0)
    m_bd = (lane2 >= GDN_DK) == (sub2 >= GDN_DK)
    eye_bd = (lane2 == sub2).astype(F32)

    def to_bd(xp):
        return jnp.where(m_bd, jnp.concatenate([xp, xp], axis=0), 0.0)

    def from_bd(xb):
        return xb[0:c, :] + xb[c:2 * c, :]

    o_parts = []
    for ci in range(GDN_NC):
        r0 = ci * c
        o_pairs = []
        for p in range(GDN_PAIRS):
            ls = slice(p * LANE, (p + 1) * LANE)
            rs = slice(r0, r0 + c)
            gcol = gc_e[rs, ls]
            grow = jnp.sum(jnp.where(m_diag, gcol, 0.0), axis=0, keepdims=True)
            diff = jnp.where(m_incl, gcol - grow, 0.0)
            dec = jnp.where(m_incl, jnp.exp(diff), 0.0)
            kn_p = kn[rs, ls]
            kstack = to_bd(kn_p).astype(BF16)
            lhs = jnp.concatenate([kb[rs, ls], qn[rs, ls]], axis=0).astype(BF16)
            aq = _dot_nt(lhs, kstack)
            a_p = jnp.where(m_strict, aq[0:c, :] * dec, 0.0)
            qk_p = aq[c:2 * c, :] * dec
            a_bd = to_bd(a_p)
            t_bd = eye_bd - a_bd
            pw = a_bd.astype(BF16)
            for _ in range(5):
                pw_f = _dot(pw, pw)
                pw = pw_f.astype(BF16)
                t_bd = t_bd + _dot(t_bd.astype(BF16), pw)
            rhs = jnp.concatenate([to_bd(vb[rs, ls]), to_bd(kbeg[rs, ls])], axis=1).astype(BF16)
            uw = _dot(t_bd.astype(BF16), rhs)
            u_p = from_bd(uw[:, 0:LANE])
            w_p = from_bd(uw[:, LANE:2 * LANE])
            s_bd = s_ref[p]
            ws = _dot(jnp.concatenate([w_p, qg[rs, ls]], axis=0).astype(BF16), s_bd.astype(BF16))
            vn_p = u_p - ws[0:c, :]
            vn_bd = to_bd(vn_p).astype(BF16)
            o_pairs.append(ws[c:2 * c, :] + _dot(qk_p.astype(BF16), vn_bd))
            glast = gc_e[r0 + c - 1:r0 + c, ls]
            kg = (kn_p * jnp.exp(glast - gcol)).astype(BF16)
            upd = _dot_tn(kg, vn_p.astype(BF16))
            s_ref[p] = s_bd * jnp.exp(glast) + jnp.where(m_bd, upd, 0.0)
        o_parts.append(jnp.concatenate(o_pairs, axis=1))
    o = jnp.concatenate(o_parts, axis=0)
    ms = segsum(o * o) * (1.0 / GDN_DV)
    z = z_ref[...]
    o_ref[...] = (o * lax.rsqrt(ms + NORM_EPS) * gn_ref[...] * (z * jax.nn.sigmoid(z))).astype(o_ref.dtype)
    for p in range(GDN_PAIRS):
        ssm_ref[p] = from_bd(s_ref[p])


def _gdn_prompt(qkv, z, ab, conv_w, a_log, dt_bias, gnorm):
    n = qkv.shape[0]
    tb = GDN_TB
    bd, lblk, eg, eb = _gdn_tables(tb)
    pad8 = lambda t: jnp.concatenate([t.reshape(1, -1), jnp.zeros((1, LANE - t.size), F32)], axis=1)
    alog128 = pad8(a_log)
    dtb128 = pad8(dt_bias)
    gn = jnp.tile(gnorm.reshape(1, GDN_DV), (1, GDN_HEADS))
    row = lambda i: (i, 0)
    const = lambda i: (0, 0)
    o, conv, ssm = pl.pallas_call(
        _gdn_kernel,
        grid=(n // tb,),
        in_specs=[pl.BlockSpec((tb, GDN_CONV_W), row), pl.BlockSpec((tb, GDN_V_W), row),
                  pl.BlockSpec((tb, LANE), row), pl.BlockSpec((GDN_CONV, GDN_CONV_W), const),
                  pl.BlockSpec((1, LANE), const), pl.BlockSpec((1, LANE), const),
                  pl.BlockSpec((1, GDN_V_W), const), pl.BlockSpec((GDN_QK_W, GDN_QK_W), const),
                  pl.BlockSpec((tb, tb), const), pl.BlockSpec((LANE, GDN_QK_W), const),
                  pl.BlockSpec((LANE, GDN_QK_W), const)],
        out_specs=[pl.BlockSpec((tb, GDN_V_W), row),
                   pl.BlockSpec((GDN_CONV - 1, GDN_CONV_W), const),
                   pl.BlockSpec((GDN_PAIRS, GDN_DK, LANE), lambda i: (0, 0, 0))],
        out_shape=[jax.ShapeDtypeStruct((n, GDN_V_W), BF16),
                   jax.ShapeDtypeStruct((GDN_CONV - 1, GDN_CONV_W), F32),
                   jax.ShapeDtypeStruct((GDN_PAIRS, GDN_DK, LANE), F32)],
        scratch_shapes=[pltpu.VMEM((tb + TAIL, GDN_CONV_W), F32),
                        pltpu.VMEM((GDN_PAIRS, LANE, LANE), F32)],
        compiler_params=_cparams(("arbitrary",)),
        name="gdn_prompt",
    )(qkv, z, ab, conv_w, alog128, dtb128, gn, bd, lblk, eg, eb)
    ssm = ssm.reshape(GDN_PAIRS, GDN_DK, 2, GDN_DV).transpose(0, 2, 1, 3).reshape(GDN_HEADS, GDN_DK, GDN_DV)
    return o, conv, ssm


def _rope_tables(pos):
    half = MLA_ROPE // 2
    inv = 1.0 / (ROPE_THETA ** (jnp.arange(half, dtype=F32) / half))
    ang = pos.astype(F32)[:, None] * inv[None, :]
    cos, sin = jnp.cos(ang), jnp.sin(ang)
    n = pos.shape[0]
    one = jnp.ones((n, ROPE_LANE0), F32)
    zero = lambda w: jnp.zeros((n, w), F32)
    c = jnp.concatenate([one, cos, cos, jnp.ones((n, LANE - ROPE_LANE0 - MLA_ROPE), F32)], axis=1)
    sa = jnp.concatenate([zero(ROPE_LANE0), -sin, zero(LANE - ROPE_LANE0 - half)], axis=1)
    sb = jnp.concatenate([zero(ROPE_LANE0 + half), sin, zero(LANE - ROPE_LANE0 - MLA_ROPE)], axis=1)
    return c, sa, sb


def _rope_apply(x, c, sa, sb):
    w = x.shape[1]
    half = MLA_ROPE // 2
    return x * c + pltpu.roll(x, w - half, 1) * sa + pltpu.roll(x, half, 1) * sb


def _tile_heads(t, n):
    return jnp.concatenate([t] * n, axis=1)


def _pack_mla_weights(w_qb, w_kvb):
    kq = w_qb.shape[0]
    wq = w_qb.reshape(kq, MLA_HEADS, MLA_QK_HEAD)
    wq = jnp.concatenate([wq, jnp.zeros((kq, MLA_HEADS, LANE - MLA_QK_HEAD), w_qb.dtype)], axis=2)
    wq = wq.reshape(kq, MLA_HEADS * LANE)
    kk = w_kvb.shape[0]
    wkv = w_kvb.reshape(kk, MLA_HEADS, MLA_NOPE + MLA_V)
    zpad = jnp.zeros((kk, MLA_HEADS, LANE - MLA_NOPE), w_kvb.dtype)
    wk = jnp.concatenate([wkv[..., :MLA_NOPE], zpad], axis=2).reshape(kk, MLA_HEADS * LANE)
    wv = jnp.concatenate([wkv[..., MLA_NOPE:], zpad], axis=2).reshape(kk, MLA_HEADS * LANE)
    wuk = jnp.transpose(wkv[..., :MLA_NOPE], (1, 2, 0))
    wuk = jnp.concatenate([wuk, jnp.zeros((MLA_HEADS, LANE - MLA_NOPE, kk), w_kvb.dtype)], axis=1)
    wuv = jnp.transpose(wkv[..., MLA_NOPE:], (1, 0, 2))
    wuv = jnp.concatenate([wuv, jnp.zeros((MLA_HEADS, kk, LANE - MLA_V), w_kvb.dtype)], axis=2)
    return wq, wk, wv, wuk, wuv


def _mla_proj_kernel(qa_ref, lat_ref, rope_ref, c_ref, sa_ref, sb_ref, qn_ref, kvn_ref, wq_ref, wk_ref, wv_ref,
                     q_ref, k_ref, v_ref, ckv_ref, kpe_ref):
    c, sa, sb = c_ref[...], sa_ref[...], sb_ref[...]
    qa = _rms(qa_ref[...], qn_ref[...]).astype(BF16)
    q = _dot(qa, wq_ref[...])
    q = _rope_apply(q, _tile_heads(c, MLA_HEADS), _tile_heads(sa, MLA_HEADS), _tile_heads(sb, MLA_HEADS))
    q_ref[...] = (q * (MLA_SCALE * LOG2E)).astype(BF16)
    ckv = _rms(lat_ref[...], kvn_ref[...])
    ckv_ref[...] = ckv
    kpe = _rope_apply(rope_ref[...], c, sa, sb)
    kpe_ref[...] = kpe
    cb = ckv.astype(BF16)
    k_ref[...] = (_dot(cb, wk_ref[...]) + _tile_heads(kpe, MLA_HEADS)).astype(BF16)
    v_ref[...] = _dot(cb, wv_ref[...]).astype(BF16)


def _mla_proj(qa, lat, rope, tabs, q_norm, kv_norm, wq, wk, wv, tm):
    n = qa.shape[0]
    hw = MLA_HEADS * LANE
    row = lambda i: (i, 0)
    const = lambda i: (0, 0)
    return pl.pallas_call(
        _mla_proj_kernel,
        grid=(n // tm,),
        in_specs=[pl.BlockSpec((tm, MLA_Q_LORA), row), pl.BlockSpec((tm, MLA_KV_LORA), row),
                  pl.BlockSpec((tm, LANE), row), pl.BlockSpec((tm, LANE), row), pl.BlockSpec((tm, LANE), row),
                  pl.BlockSpec((tm, LANE), row), pl.BlockSpec((1, MLA_Q_LORA), const),
                  pl.BlockSpec((1, MLA_KV_LORA), const), pl.BlockSpec((MLA_Q_LORA, hw), const),
                  pl.BlockSpec((MLA_KV_LORA, hw), const), pl.BlockSpec((MLA_KV_LORA, hw), const)],
        out_specs=[pl.BlockSpec((tm, hw), row), pl.BlockSpec((tm, hw), row), pl.BlockSpec((tm, hw), row),
                   pl.BlockSpec((tm, MLA_KV_LORA), row), pl.BlockSpec((tm, LANE), row)],
        out_shape=[jax.ShapeDtypeStruct((n, hw), BF16), jax.ShapeDtypeStruct((n, hw), BF16),
                   jax.ShapeDtypeStruct((n, hw), BF16), jax.ShapeDtypeStruct((n, MLA_KV_LORA), F32),
                   jax.ShapeDtypeStruct((n, LANE), F32)],
        compiler_params=_cparams(("arbitrary",)),
        name="mla_proj",
    )(qa, lat, rope, *tabs, q_norm, kv_norm, wq, wk, wv)


ATT_TQ = 1024
ATT_TK = 512
NEG_BIG = -1e30


def _attn_kernel(qi_ref, kj_ref, q_ref, k_ref, v_ref, o_ref, acc_ref, m_ref, l_ref):
    step = pl.program_id(0)
    i = qi_ref[step]
    j = kj_ref[step]
    tq, tk = ATT_TQ, ATT_TK
    delta = i * tq - j * tk
    last_j = (i * tq + tq - 1) // tk

    @pl.when(j == 0)
    def _():
        acc_ref[...] = jnp.zeros_like(acc_ref)
        m_ref[...] = jnp.full_like(m_ref, NEG_BIG)
        l_ref[...] = jnp.zeros_like(l_ref)

    def update(masked):
        if masked:
            keep = (lax.broadcasted_iota(I32, (tq, tk), 1) - lax.broadcasted_iota(I32, (tq, tk), 0)) <= delta
        for h in range(MLA_HEADS):
            hs = slice(h * LANE, (h + 1) * LANE)
            s = _dot_nt(q_ref[:, hs], k_ref[:, hs])
            if masked:
                s = jnp.where(keep, s, NEG_BIG)
            m_prev = m_ref[h]
            m_new = jnp.maximum(m_prev, jnp.max(s, axis=1, keepdims=True))
            alpha = jnp.exp2(m_prev - m_new)
            p = jnp.exp2(s - m_new[:, 0:1])
            l_ref[h] = alpha * l_ref[h] + jnp.sum(p, axis=1, keepdims=True)
            acc_ref[:, hs] = alpha * acc_ref[:, hs] + _dot(p.astype(BF16), v_ref[:, hs])
            m_ref[h] = m_new

    fully_visible = (j * tk + tk - 1) <= i * tq

    @pl.when(fully_visible)
    def _():
        update(False)

    @pl.when(jnp.logical_not(fully_visible))
    def _():
        update(True)

    @pl.when(j == last_j)
    def _():
        for h in range(MLA_HEADS):
            hs = slice(h * LANE, (h + 1) * LANE)
            o_ref[:, hs] = (acc_ref[:, hs] / l_ref[h]).astype(o_ref.dtype)


def _attn_prompt(q, k, v):
    n = q.shape[0]
    tq, tk = ATT_TQ, ATT_TK
    nq = n // tq
    nk_of = [(i * tq + tq - 1) // tk + 1 for i in range(nq)]
    qi = np.concatenate([np.full(nk_of[i], i) for i in range(nq)]).astype(np.int32)
    kj = np.concatenate([np.arange(nk_of[i]) for i in range(nq)]).astype(np.int32)
    hw = MLA_HEADS * LANE
    grid_spec = pltpu.PrefetchScalarGridSpec(
        num_scalar_prefetch=2,
        grid=(len(qi),),
        in_specs=[pl.BlockSpec((tq, hw), lambda s, qi, kj: (qi[s], 0)),
                  pl.BlockSpec((tk, hw), lambda s, qi, kj: (kj[s], 0)),
                  pl.BlockSpec((tk, hw), lambda s, qi, kj: (kj[s], 0))],
        out_specs=pl.BlockSpec((tq, hw), lambda s, qi, kj: (qi[s], 0)),
        scratch_shapes=[pltpu.VMEM((tq, hw), F32), pltpu.VMEM((MLA_HEADS, tq, LANE), F32),
                        pltpu.VMEM((MLA_HEADS, tq, LANE), F32)],
    )
    return pl.pallas_call(
        _attn_kernel,
        grid_spec=grid_spec,
        out_shape=jax.ShapeDtypeStruct((n, hw), BF16),
        compiler_params=_cparams(("arbitrary",)),
        name="mla_attn_prompt",
    )(jnp.asarray(qi), jnp.asarray(kj), q, k, v)


def _memkv_kernel(mem_ref, g_ref, wk_ref, wv_ref, k_ref, v_ref):
    m = _rms(mem_ref[...], g_ref[...]).astype(BF16)
    k_ref[...] = _dot(m, wk_ref[...])
    v_ref[...] = _dot(m, wv_ref[...])


def _memkv(mem, g, wk, wv):
    m = mem.shape[0]
    return pl.pallas_call(
        _memkv_kernel,
        out_shape=[jax.ShapeDtypeStruct((m, MEM_W), F32), jax.ShapeDtypeStruct((m, MEM_W), F32)],
        compiler_params=pltpu.CompilerParams(vmem_limit_bytes=VMEM_LIMIT),
        name="mem_kv",
    )(mem, g, wk, wv)


POST_TM = 256


def _merge_x1(x, og, om, sga, sgb, wgu, wmu, wout):
    ga = sga.astype(F32) * _dot(og, wgu)
    gb = sgb.astype(F32) * _dot(om, wmu)
    return x + _dot((ga + gb).astype(BF16), wout)


def _mem_attend_shared(qb, mk, mv):
    outs = []
    for h in range(MEM_HEADS):
        hs = slice(h * MEM_HEAD_DIM, (h + 1) * MEM_HEAD_DIM)
        s = _dot_nt(qb[:, hs], mk[:, hs])
        p = jnp.exp2(s - jnp.max(s, axis=1, keepdims=True))
        l = jnp.sum(p, axis=1, keepdims=True)
        outs.append(_dot(p.astype(BF16), mv[:, hs]) / l)
    return jnp.concatenate(outs, axis=1)


def _router(h3, rw, rb, base, lstrict):
    tm = h3.shape[0]
    logits = jnp.dot(h3, rw, precision=lax.Precision.HIGHEST, preferred_element_type=F32) + rb
    lane = lax.broadcasted_iota(I32, (tm, LANE), 1)
    lane_f = lane.astype(F32)
    work = jnp.where(lane < N_EXPERTS, logits, -jnp.inf)
    vals, ohs, idxs = [], [], []
    for _ in range(TOP_K):
        mx = jnp.max(work, axis=1, keepdims=True)
        idx = jnp.min(jnp.where(work == mx, lane_f, float(LANE)), axis=1, keepdims=True)
        oh = lane_f == idx
        work = jnp.where(oh, -jnp.inf, work)
        vals.append(mx)
        idxs.append(idx)
        ohs.append(oh)
    es = [jnp.exp(v - vals[0]) for v in vals]
    denom = es[0] + es[1] + es[2] + es[3]
    sel = jnp.zeros((tm, LANE), F32)
    for oh in ohs:
        sel = sel + oh.astype(F32)
    rank_te = _dot(lstrict, sel.astype(BF16)) + base
    idx_out = jnp.zeros((tm, LANE), F32)
    w_out = jnp.zeros((tm, LANE), F32)
    r_out = jnp.zeros((tm, LANE), F32)
    for k in range(TOP_K):
        rk = jnp.sum(jnp.where(ohs[k], rank_te, 0.0), axis=1, keepdims=True)
        idx_out = jnp.where(lane == k, idxs[k], idx_out)
        w_out = jnp.where(lane == k, es[k] / denom, w_out)
        r_out = jnp.where(lane == k, rk, r_out)
    new_base = base + jnp.sum(sel, axis=0, keepdims=True)
    return idx_out.astype(I32), w_out, r_out.astype(I32), new_base


def _post_prompt_kernel(x_ref, og_ref, om_ref, sga_ref, sgb_ref, wgu_ref, wmu_ref, wout_ref, nmem_ref, wmq_ref,
                        mk_ref, mv_ref, wmo_ref, nffn_ref, rw_ref, rb_ref, base0_ref, ls_ref,
                        x2_ref, h3_ref, idx_ref, wts_ref, rank_ref, cnt_ref, base_ref):
    @pl.when(pl.program_id(0) == 0)
    def _():
        base_ref[...] = base0_ref[...]

    x1 = _merge_x1(x_ref[...], og_ref[...], om_ref[...], sga_ref[...], sgb_ref[...],
                   wgu_ref[...], wmu_ref[...], wout_ref[...])
    h2 = _rms(x1, nmem_ref[...]).astype(BF16)
    qb = (_dot(h2, wmq_ref[...]) * (MEM_HEAD_DIM ** -0.5 * LOG2E)).astype(BF16)
    om = _mem_attend_shared(qb, mk_ref[...], mv_ref[...])
    x2 = x1 + _dot(om.astype(BF16), wmo_ref[...])
    x2_ref[...] = x2
    h3 = _rms(x2, nffn_ref[...])
    h3_ref[...] = h3
    idx, wts, rank, nb = _router(h3, rw_ref[...], rb_ref[...], base_ref[...], ls_ref[...])
    idx_ref[...] = idx
    wts_ref[...] = wts
    rank_ref[...] = rank
    base_ref[...] = nb
    cnt_ref[...] = nb


def _strict_lower(tm):
    r = np.arange(tm)
    return jnp.asarray((r[:, None] > r[None, :]).astype(np.float32), BF16)


def _post_prompt(x, og, om, sga, sgb, wgu, wmu, wout, nmem, wmq, mk, mv, wmo, nffn, rw, rb, base0):
    n = x.shape[0]
    tm = POST_TM
    row = lambda i: (i, 0)
    const = lambda i: (0, 0)
    full = lambda a: pl.BlockSpec(a.shape, const)
    ls = _strict_lower(tm)
    ins = [x, og, om, sga, sgb, wgu, wmu, wout, nmem, wmq, mk, mv, wmo, nffn, rw, rb, base0, ls]
    in_specs = [pl.BlockSpec((tm, a.shape[1]), row) for a in ins[:5]] + [full(a) for a in ins[5:]]
    return pl.pallas_call(
        _post_prompt_kernel,
        grid=(n // tm,),
        in_specs=in_specs,
        out_specs=[pl.BlockSpec((tm, D_MODEL), row), pl.BlockSpec((tm, D_MODEL), row),
                   pl.BlockSpec((tm, LANE), row), pl.BlockSpec((tm, LANE), row), pl.BlockSpec((tm, LANE), row),
                   pl.BlockSpec((1, LANE), const)],
        out_shape=[jax.ShapeDtypeStruct((n, D_MODEL), F32), jax.ShapeDtypeStruct((n, D_MODEL), F32),
                   jax.ShapeDtypeStruct((n, LANE), I32), jax.ShapeDtypeStruct((n, LANE), F32),
                   jax.ShapeDtypeStruct((n, LANE), I32), jax.ShapeDtypeStruct((1, LANE), F32)],
        scratch_shapes=[pltpu.VMEM((1, LANE), F32)],
        compiler_params=_cparams(("arbitrary",)),
        name="post_prompt",
    )(*ins)


MOE_TE = 256
MOE_TM = 256


def _route_plan(counts, n_rows):
    te = MOE_TE
    padded = ((counts + te - 1) // te) * te
    ends = jnp.cumsum(padded)
    offs = ends - padded
    n_tiles = n_rows // te + N_EXPERTS
    tile_start = jnp.arange(n_tiles, dtype=I32) * te
    tile_e = jnp.minimum(jnp.searchsorted(ends, tile_start, side="right").astype(I32), N_EXPERTS - 1)
    tile_nv = jnp.clip(offs[tile_e] + counts[tile_e] - tile_start, 0, te).astype(I32)
    return offs.astype(I32), tile_e, tile_nv, n_tiles


def _dispatch_kernel(pos_ref, h_ref, xs_in_ref, xs_ref, sem):
    del xs_in_ref
    tm = h_ref.shape[0]

    def body(r, carry):
        for k in range(TOP_K):
            p = pos_ref[TOP_K * r + k]
            pltpu.make_async_copy(h_ref.at[pl.ds(r, 1)], xs_ref.at[pl.ds(p, 1)], sem).start()
        return carry

    lax.fori_loop(0, tm, body, 0)
    for _ in range(TOP_K):
        pltpu.make_async_copy(h_ref, xs_ref.at[pl.ds(0, tm)], sem).wait()


def _dispatch(pos, h3, xs, tm):
    n = h3.shape[0]
    return pl.pallas_call(
        _dispatch_kernel,
        grid=(n // tm,),
        in_specs=[pl.BlockSpec((tm * TOP_K,), lambda i: (i,), memory_space=pltpu.SMEM),
                  pl.BlockSpec((tm, D_MODEL), lambda i: (i, 0)),
                  pl.BlockSpec(memory_space=pl.ANY)],
        out_specs=pl.BlockSpec(memory_space=pl.ANY),
        out_shape=jax.ShapeDtypeStruct(xs.shape, xs.dtype),
        scratch_shapes=[pltpu.SemaphoreType.DMA],
        input_output_aliases={2: 0},
        compiler_params=_cparams(("arbitrary",)),
        name="moe_dispatch",
    )(pos, h3, xs)


def _ffn_kernel(te_ref, nv_ref, x_ref, wg_ref, bg_ref, wu_ref, bu_ref, wd_ref, bd_ref, y_ref,
                wgb_ref, wub_ref, wdb_ref):
    t = pl.program_id(0)
    e = te_ref[t]
    nv = nv_ref[t]
    changed = jnp.logical_or(t == 0, te_ref[jnp.maximum(t - 1, 0)] != e)

    @pl.when(changed)
    def _():
        wgb_ref[...] = wg_ref[...].astype(BF16)
        wub_ref[...] = wu_ref[...].astype(BF16)
        wdb_ref[...] = wd_ref[...].astype(BF16)

    @pl.when(nv > 0)
    def _():
        rows = lax.broadcasted_iota(I32, x_ref.shape, 0)
        xb = jnp.where(rows < nv, x_ref[...], 0.0).astype(BF16)
        gate = jnp.minimum(_dot(xb, wgb_ref[...]) + bg_ref[...], SWIGLU_LIMIT)
        up = jnp.clip(_dot(xb, wub_ref[...]) + bu_ref[...], -SWIGLU_LIMIT, SWIGLU_LIMIT)
        act = (up + 1.0) * gate * jax.nn.sigmoid(SWIGLU_ALPHA * gate)
        y_ref[...] = _dot(act.astype(BF16), wdb_ref[...]) + bd_ref[...]

    @pl.when(nv == 0)
    def _():
        y_ref[...] = jnp.zeros_like(y_ref)


def _ffn(tile_e, tile_nv, xs, w_gate, b_gate, w_up, b_up, w_down, b_down, n_tiles):
    te = MOE_TE
    wspec = lambda: pl.BlockSpec((None, D_MODEL, D_FF), lambda t, te_, nv_: (te_[t], 0, 0))
    bspec = lambda: pl.BlockSpec((None, 1, D_FF), lambda t, te_, nv_: (te_[t], 0, 0))
    grid_spec = pltpu.PrefetchScalarGridSpec(
        num_scalar_prefetch=2,
        grid=(n_tiles,),
        in_specs=[pl.BlockSpec((te, D_MODEL), lambda t, te_, nv_: (t, 0)),
                  wspec(), bspec(), wspec(), bspec(), wspec(), bspec()],
        out_specs=pl.BlockSpec((te, D_MODEL), lambda t, te_, nv_: (t, 0)),
        scratch_shapes=[pltpu.VMEM((D_MODEL, D_FF), BF16), pltpu.VMEM((D_MODEL, D_FF), BF16),
                        pltpu.VMEM((D_FF, D_MODEL), BF16)],
    )
    return pl.pallas_call(
        _ffn_kernel,
        grid_spec=grid_spec,
        out_shape=jax.ShapeDtypeStruct(xs.shape, F32),
        compiler_params=_cparams(("arbitrary",)),
        name="moe_ffn",
    )(tile_e, tile_nv, xs, w_gate, b_gate.reshape(N_EXPERTS, 1, D_FF), w_up, b_up.reshape(N_EXPERTS, 1, D_FF),
      w_down, b_down.reshape(N_EXPERTS, 1, D_MODEL))


def _combine_kernel(pos_ref, x2_ref, w_ref, nf_ref, ys_ref, y_ref, buf_ref, sem):
    tm = x2_ref.shape[0]

    def body(r, carry):
        for k in range(TOP_K):
            p = pos_ref[TOP_K * r + k]
            pltpu.make_async_copy(ys_ref.at[pl.ds(p, 1)], buf_ref.at[k, pl.ds(r, 1)], sem).start()
        return carry

    lax.fori_loop(0, tm, body, 0)
    for k in range(TOP_K):
        pltpu.make_async_copy(ys_ref.at[pl.ds(0, tm)], buf_ref.at[k], sem).wait()
    w = w_ref[...]
    y = x2_ref[...]
    for k in range(TOP_K):
        y = y + w[:, k:k + 1] * buf_ref[k]
    y_ref[...] = _rms(y, nf_ref[...])


def _combine(pos, x2, wts, nf, ys, tm):
    n = x2.shape[0]
    return pl.pallas_call(
        _combine_kernel,
        grid=(n // tm,),
        in_specs=[pl.BlockSpec((tm * TOP_K,), lambda i: (i,), memory_space=pltpu.SMEM),
                  pl.BlockSpec((tm, D_MODEL), lambda i: (i, 0)),
                  pl.BlockSpec((tm, LANE), lambda i: (i, 0)),
                  pl.BlockSpec((1, D_MODEL), lambda i: (0, 0)),
                  pl.BlockSpec(memory_space=pl.ANY)],
        out_specs=pl.BlockSpec((tm, D_MODEL), lambda i: (i, 0)),
        out_shape=jax.ShapeDtypeStruct((n, D_MODEL), F32),
        scratch_shapes=[pltpu.VMEM((TOP_K, tm, D_MODEL), F32), pltpu.SemaphoreType.DMA],
        compiler_params=_cparams(("arbitrary",)),
        name="moe_combine",
    )(pos, x2, wts, nf, ys)


def _gdn_dec_prep_kernel(qkv_ref, cs_ref, z_ref, ab_ref, cw_ref, alog_ref, dtb_ref, bd_ref, eg_ref, eb_ref,
                         csn_ref, q_ref, k_ref, v_ref, ege_ref, be_ref, zs_ref):
    x = qkv_ref[...]
    y = cw_ref[GDN_CONV - 1:GDN_CONV, :] * x
    for j in range(GDN_CONV - 1):
        y = y + cw_ref[j:j + 1, :] * cs_ref[:, j, :]
    for j in range(GDN_CONV - 2):
        csn_ref[:, j, :] = cs_ref[:, j + 1, :]
    csn_ref[:, GDN_CONV - 2, :] = x
    act = y * jax.nn.sigmoid(y)
    q = act[:, 0:GDN_QK_W]
    k = act[:, GDN_QK_W:2 * GDN_QK_W]
    bd = bd_ref[...]
    q_ref[...] = q * lax.rsqrt(_dot_lhs_exact(q * q, bd) + NORM_EPS) * (GDN_DK ** -0.5)
    k_ref[...] = k * lax.rsqrt(_dot_lhs_exact(k * k, bd) + NORM_EPS)
    v_ref[...] = act[:, 2 * GDN_QK_W:]
    ab = ab_ref[...]
    g128 = -jnp.exp(alog_ref[...]) * _softplus(ab + dtb_ref[...])
    ege_ref[...] = jnp.exp(_dot_lhs_exact(g128, eg_ref[...]))
    be_ref[...] = _dot_lhs_exact(jax.nn.sigmoid(ab), eb_ref[...])
    z = z_ref[...]
    zs_ref[...] = z * jax.nn.sigmoid(z)


def _gdn_dec_prep(qkv, cs, z, ab, conv_w, alog128, dtb128):
    b = qkv.shape[0]
    bd, _, eg, eb = _gdn_tables(GDN_CHUNK)
    wide = jax.ShapeDtypeStruct((b, GDN_QK_W), F32)
    return pl.pallas_call(
        _gdn_dec_prep_kernel,
        out_shape=[jax.ShapeDtypeStruct(cs.shape, F32)] + [wide] * 6,
        compiler_params=pltpu.CompilerParams(vmem_limit_bytes=VMEM_LIMIT),
        name="gdn_decode_prep",
    )(qkv, cs, z, ab, conv_w, alog128, dtb128, bd, eg, eb)


GDN_DEC_B = 64


def _gdn_dec_step_kernel(s_ref, q_ref, k_ref, v_ref, eg_ref, be_ref, zs_ref, gn_ref, sn_ref, o_ref):
    s = s_ref[...]
    q, k, v = q_ref[...], k_ref[...], v_ref[...]
    eg, be = eg_ref[...], be_ref[...]
    eye = lax.broadcasted_iota(I32, (GDN_DK, GDN_DK), 0) == lax.broadcasted_iota(I32, (GDN_DK, GDN_DK), 1)
    kcol = jnp.sum(jnp.where(eye, k, 0.0), axis=2, keepdims=True)
    qcol = jnp.sum(jnp.where(eye, q, 0.0), axis=2, keepdims=True)
    ks = jnp.sum(s * kcol, axis=1, keepdims=True)
    qs = jnp.sum(s * qcol, axis=1, keepdims=True)
    vn = be * (v - eg * ks)
    qk = jnp.sum(q * k, axis=2, keepdims=True)
    o = eg * qs + qk * vn
    sn_ref[...] = s * eg + kcol * vn
    ms = jnp.mean(o * o, axis=2, keepdims=True)
    o_ref[...] = o * lax.rsqrt(ms + NORM_EPS) * gn_ref[...] * zs_ref[...]


def _gdn_dec_step(s0, rows, gn):
    bh = s0.shape[0]
    bb = GDN_DEC_B
    sspec = pl.BlockSpec((bb, GDN_DK, GDN_DV), lambda i: (i, 0, 0))
    rspec = pl.BlockSpec((bb, 1, GDN_DV), lambda i: (i, 0, 0))
    return pl.pallas_call(
        _gdn_dec_step_kernel,
        grid=(bh // bb,),
        in_specs=[sspec] + [rspec] * 6 + [pl.BlockSpec((1, 1, GDN_DV), lambda i: (0, 0, 0))],
        out_specs=[sspec, rspec],
        out_shape=[jax.ShapeDtypeStruct(s0.shape, F32), jax.ShapeDtypeStruct((bh, 1, GDN_DV), F32)],
        compiler_params=_cparams(("arbitrary",)),
        name="gdn_decode_step",
    )(s0, *rows, gn)


def _mla_dec_proj_kernel(qa_ref, lat_ref, rope_ref, c_ref, sa_ref, sb_ref, qn_ref, kvn_ref, wq_ref, wuk_ref,
                         qlat_ref, q_ref, ckv_ref, kpe_ref):
    c, sa, sb = c_ref[...], sa_ref[...], sb_ref[...]
    q = _dot_f32(_rms(qa_ref[...], qn_ref[...]), wq_ref[...])
    q = _rope_apply(q, _tile_heads(c, MLA_HEADS), _tile_heads(sa, MLA_HEADS), _tile_heads(sb, MLA_HEADS))
    q = q * (MLA_SCALE * LOG2E)
    q_ref[...] = q
    for h in range(MLA_HEADS):
        qlat_ref[:, h * MLA_KV_LORA:(h + 1) * MLA_KV_LORA] = _dot_f32(q[:, h * LANE:(h + 1) * LANE], wuk_ref[h])
    ckv_ref[...] = _rms(lat_ref[...], kvn_ref[...])
    kpe_ref[...] = _rope_apply(rope_ref[...], c, sa, sb)


def _mla_dec_proj(qa, lat, rope, tabs, q_norm, kv_norm, wq, wuk):
    b = qa.shape[0]
    return pl.pallas_call(
        _mla_dec_proj_kernel,
        out_shape=[jax.ShapeDtypeStruct((b, MLA_HEADS * MLA_KV_LORA), F32),
                   jax.ShapeDtypeStruct((b, MLA_HEADS * LANE), F32),
                   jax.ShapeDtypeStruct((b, MLA_KV_LORA), F32), jax.ShapeDtypeStruct((b, LANE), F32)],
        compiler_params=pltpu.CompilerParams(vmem_limit_bytes=VMEM_LIMIT),
        name="mla_decode_proj",
    )(qa, lat, rope, *tabs, q_norm, kv_norm, wq, wuk)


PAGE_GROUP = 8


def _paged_attn_kernel(pt_ref, qlat_ref, qpe_ref, ckv_ref, kpe_ref, lat_hbm, rope_hbm, o_ref,
                       latbuf, ropebuf, sem):
    b = pl.program_id(0)
    nb = pl.num_programs(0)
    n_pages = pt_ref.shape[0] // nb
    ngroups = n_pages // PAGE_GROUP
    total = nb * ngroups

    def copies(gidx, slot):
        out = []
        for g in range(PAGE_GROUP):
            page = pt_ref[gidx * PAGE_GROUP + g]
            out.append(pltpu.make_async_copy(lat_hbm.at[page], latbuf.at[slot, g], sem.at[slot]))
            out.append(pltpu.make_async_copy(rope_hbm.at[page], ropebuf.at[slot, g], sem.at[slot]))
        return out

    @pl.when(b == 0)
    def _():
        for cp in copies(0, 0):
            cp.start()

    q = qlat_ref[0]
    qp = qpe_ref[0]
    nh = MLA_HEADS

    def hi_lo(x):
        hi = x.astype(BF16)
        return jnp.concatenate([hi, (x - hi.astype(F32)).astype(BF16)], axis=0)

    q2, qp2 = hi_lo(q), hi_lo(qp)

    def body(t, carry):
        m, l, acc = carry
        gidx = b * ngroups + t
        slot = lax.rem(gidx, 2)

        @pl.when(gidx + 1 < total)
        def _():
            for cp in copies(gidx + 1, 1 - slot):
                cp.start()

        for cp in copies(gidx, slot):
            cp.wait()
        lats, ss = [], []
        for g in range(PAGE_GROUP):
            latg = latbuf[slot, g].astype(BF16)
            ropeg = ropebuf[slot, g].astype(BF16)
            ss.append(_dot_nt(q2, latg) + _dot_nt(qp2, ropeg))
            lats.append(latg)
        s2 = jnp.concatenate(ss, axis=1)
        s = s2[0:nh] + s2[nh:2 * nh]
        m_new = jnp.maximum(m, jnp.max(s, axis=1, keepdims=True))
        alpha = jnp.exp2(m - m_new)
        p = jnp.exp2(s - m_new)
        l = alpha * l + jnp.sum(p, axis=1, keepdims=True)
        pb = hi_lo(p)
        pv = _dot(pb[:, 0:PAGE_SIZE], lats[0])
        for g in range(1, PAGE_GROUP):
            pv = pv + _dot(pb[:, g * PAGE_SIZE:(g + 1) * PAGE_SIZE], lats[g])
        return m_new, l, alpha * acc + pv[0:nh] + pv[nh:2 * nh]

    init = (jnp.full((MLA_HEADS, 1), NEG_BIG, F32), jnp.zeros((MLA_HEADS, 1), F32),
            jnp.zeros((MLA_HEADS, MLA_KV_LORA), F32))
    m, l, acc = lax.fori_loop(0, ngroups, body, init)
    ck = ckv_ref[0]
    kp = kpe_ref[0]
    s_new = jnp.sum(q * ck, axis=1, keepdims=True) + jnp.sum(qp * kp, axis=1, keepdims=True)
    m_new = jnp.maximum(m, s_new)
    alpha = jnp.exp2(m - m_new)
    p_new = jnp.exp2(s_new - m_new)
    l = alpha * l + p_new
    o_ref[0] = (alpha * acc + p_new * ck) / l


def _paged_attn(page_table, qlat, qpe, ckv, kpe, pool_lat, pool_rope):
    b, n_pages = page_table.shape
    grid_spec = pltpu.PrefetchScalarGridSpec(
        num_scalar_prefetch=1,
        grid=(b,),
        in_specs=[pl.BlockSpec((1, MLA_HEADS, MLA_KV_LORA), lambda i, pt: (i, 0, 0)),
                  pl.BlockSpec((1, MLA_HEADS, MLA_ROPE), lambda i, pt: (i, 0, 0)),
                  pl.BlockSpec((1, 1, MLA_KV_LORA), lambda i, pt: (i, 0, 0)),
                  pl.BlockSpec((1, 1, MLA_ROPE), lambda i, pt: (i, 0, 0)),
                  pl.BlockSpec(memory_space=pl.ANY), pl.BlockSpec(memory_space=pl.ANY)],
        out_specs=pl.BlockSpec((1, MLA_HEADS, MLA_KV_LORA), lambda i, pt: (i, 0, 0)),
        scratch_shapes=[pltpu.VMEM((2, PAGE_GROUP, PAGE_SIZE, MLA_KV_LORA), F32),
                        pltpu.VMEM((2, PAGE_GROUP, PAGE_SIZE, MLA_ROPE), F32),
                        pltpu.SemaphoreType.DMA((2,))],
    )
    return pl.pallas_call(
        _paged_attn_kernel,
        grid_spec=grid_spec,
        out_shape=jax.ShapeDtypeStruct((b, MLA_HEADS, MLA_KV_LORA), F32),
        compiler_params=_cparams(("arbitrary",)),
        name="mla_paged_attn",
    )(page_table.reshape(-1), qlat, qpe, ckv, kpe, pool_lat, pool_rope)


MEM_DEC_B = 8


def _mem_dec_kernel(q_ref, mk_ref, mv_ref, o_ref):
    q = q_ref[...]
    outs = []
    for h in range(MEM_HEADS):
        hs = slice(h * MEM_HEAD_DIM, (h + 1) * MEM_HEAD_DIM)
        s = jnp.sum(mk_ref[:, :, hs] * q[:, :, hs], axis=2, keepdims=True)
        p = jnp.exp(s - jnp.max(s, axis=1, keepdims=True))
        l = jnp.sum(p, axis=1, keepdims=True)
        outs.append(jnp.sum(p * mv_ref[:, :, hs], axis=1, keepdims=True) / l)
    o_ref[...] = jnp.concatenate(outs, axis=2)


def _mem_dec(q, mk, mv):
    b = q.shape[0]
    bb = MEM_DEC_B
    qspec = pl.BlockSpec((bb, 1, MEM_W), lambda i: (i, 0, 0))
    kspec = pl.BlockSpec((bb, MEM_TOKENS, MEM_W), lambda i: (i, 0, 0))
    return pl.pallas_call(
        _mem_dec_kernel,
        grid=(b // bb,),
        in_specs=[qspec, kspec, kspec],
        out_specs=qspec,
        out_shape=jax.ShapeDtypeStruct((b, 1, MEM_W), F32),
        compiler_params=_cparams(("arbitrary",)),
        name="mem_attn_decode",
    )(q, mk, mv)


def _post_a_sample_kernel(x_ref, og_ref, olat_ref, sga_ref, sgb_ref, wuv_ref, wgu_ref, wmu_ref, wout_ref,
                          nmem_ref, wmq_ref, x1_ref, q_ref):
    olat = olat_ref[...]
    om = jnp.concatenate([_dot_f32(olat[:, h * MLA_KV_LORA:(h + 1) * MLA_KV_LORA], wuv_ref[h])
                          for h in range(MLA_HEADS)], axis=1)
    ga = sga_ref[...] * _dot_f32(og_ref[...], wgu_ref[...])
    gb = sgb_ref[...] * _dot_f32(om, wmu_ref[...])
    x1 = x_ref[...] + _dot_f32(ga + gb, wout_ref[...])
    x1_ref[...] = x1
    q_ref[...] = _dot_f32(_rms(x1, nmem_ref[...]), wmq_ref[...]) * (MEM_HEAD_DIM ** -0.5)


def _post_a_sample(x, og, olat, sga, sgb, wuv, wgu, wmu, wout, nmem, wmq):
    b = x.shape[0]
    return pl.pallas_call(
        _post_a_sample_kernel,
        out_shape=[jax.ShapeDtypeStruct((b, D_MODEL), F32), jax.ShapeDtypeStruct((b, MEM_W), F32)],
        compiler_params=pltpu.CompilerParams(vmem_limit_bytes=VMEM_LIMIT),
        name="post_a_sample",
    )(x, og, olat, sga, sgb, wuv, wgu, wmu, wout, nmem, wmq)


def _post_b_sample_kernel(x1_ref, om_ref, wmo_ref, nffn_ref, rw_ref, rb_ref, base0_ref, ls_ref,
                          x2_ref, h3_ref, idx_ref, wts_ref, rank_ref, cnt_ref):
    x2 = x1_ref[...] + _dot_f32(om_ref[...], wmo_ref[...])
    x2_ref[...] = x2
    h3 = _rms(x2, nffn_ref[...])
    h3_ref[...] = h3
    idx, wts, rank, nb = _router(h3, rw_ref[...], rb_ref[...], base0_ref[...], ls_ref[...])
    idx_ref[...] = idx
    wts_ref[...] = wts
    rank_ref[...] = rank
    cnt_ref[...] = nb


def _post_b_sample(x1, om, wmo, nffn, rw, rb, base0):
    b = x1.shape[0]
    return pl.pallas_call(
        _post_b_sample_kernel,
        out_shape=[jax.ShapeDtypeStruct((b, D_MODEL), F32), jax.ShapeDtypeStruct((b, D_MODEL), F32),
                   jax.ShapeDtypeStruct((b, LANE), I32), jax.ShapeDtypeStruct((b, LANE), F32),
                   jax.ShapeDtypeStruct((b, LANE), I32), jax.ShapeDtypeStruct((1, LANE), F32)],
        compiler_params=pltpu.CompilerParams(vmem_limit_bytes=VMEM_LIMIT),
        name="post_b_sample",
    )(x1, om, wmo, nffn, rw, rb, base0, _strict_lower(b))


def _pad_lanes(t, width=LANE):
    t = t.reshape(1, -1)
    return jnp.concatenate([t, jnp.zeros((1, width - t.shape[1]), t.dtype)], axis=1)


def kernel(x_prompt, x_sample, mem_prompt, cache_kv_latent, cache_k_rope, cache_mem_k, cache_mem_v,
           state_gdn_conv, state_gdn_ssm, page_table,
           norm_mix, w_in, gdn_conv_w, gdn_a_log, gdn_dt_bias, gdn_norm, gdn_w_up,
           mla_q_norm, mla_w_qb, mla_kv_norm, mla_w_kvb, mla_w_up, w_out,
           norm_mem, mem_in_norm, w_mq, w_mk, w_mv, w_mo,
           norm_ffn, router_w, router_b, w_gate, b_gate, w_up, b_up, w_down, b_down, norm_final):
    depth = w_in.shape[0]
    assert depth == 1, "single-layer trunk"
    l = 0
    n = x_prompt.shape[1]
    b = x_sample.shape[0]
    past_len = page_table.shape[1] * PAGE_SIZE
    xp = x_prompt.reshape(n, D_MODEL)
    xs = x_sample.reshape(b, D_MODEL)
    row = lambda t: t.reshape(1, -1)

    wp32 = _pack_w_in(w_in[l])
    wp16 = wp32.astype(BF16)
    wq, wk, wv, wuk, wuv = _pack_mla_weights(mla_w_qb[l], mla_w_kvb[l])
    wmu32 = mla_w_up[l].reshape(MLA_HEADS, MLA_V, D_MODEL)
    wmu32 = jnp.concatenate([wmu32, jnp.zeros((MLA_HEADS, LANE - MLA_V, D_MODEL), F32)],
                            axis=1).reshape(MLA_HEADS * LANE, D_MODEL)
    rw = jnp.concatenate([router_w[l], jnp.zeros((D_MODEL, LANE - N_EXPERTS), F32)], axis=1)
    rb = _pad_lanes(router_b[l])
    alog128 = _pad_lanes(gdn_a_log[l])
    dtb128 = _pad_lanes(gdn_dt_bias[l])

    mk_p, mv_p = _memkv(mem_prompt.reshape(MEM_TOKENS, D_MODEL), row(mem_in_norm[l]),
                        w_mk[l].astype(BF16), w_mv[l].astype(BF16))
    qkv, z, ab, qa, lat, rope, sga, sgb = _inproj(xp, row(norm_mix[l]), wp16, 256)
    og_p, conv_p, ssm_p = _gdn_prompt(qkv, z, ab, gdn_conv_w[l], gdn_a_log[l], gdn_dt_bias[l], gdn_norm[l])
    tabs_p = _rope_tables(jnp.arange(n))
    q, k, v, ckv_p, kpe_p = _mla_proj(qa, lat, rope, tabs_p, row(mla_q_norm[l]), row(mla_kv_norm[l]),
                                      wq.astype(BF16), wk.astype(BF16), wv.astype(BF16), 256)
    om_p = _attn_prompt(q, k, v)
    base0 = jnp.zeros((1, LANE), F32)
    x2_p, h3_p, idx_p, wts_p, rank_p, cnt_p = _post_prompt(
        xp, og_p, om_p, sga, sgb, gdn_w_up[l].astype(BF16), wmu32.astype(BF16), w_out[l].astype(BF16),
        row(norm_mem[l]), w_mq[l].astype(BF16), mk_p.astype(BF16), mv_p.astype(BF16), w_mo[l].astype(BF16),
        row(norm_ffn[l]), rw, rb, base0)

    qkv_s, z_s, ab_s, qa_s, lat_s, rope_s, sga_s, sgb_s = _inproj(xs, row(norm_mix[l]), wp32, b, precise=True)
    conv_s, qn_s, kn_s, v_s, ege_s, be_s, zs_s = _gdn_dec_prep(
        qkv_s, state_gdn_conv[l], z_s, ab_s, gdn_conv_w[l], alog128, dtb128)
    per_head = lambda t: t.reshape(b * GDN_HEADS, 1, GDN_DV)
    ssm_s, og_s = _gdn_dec_step(
        state_gdn_ssm[l].reshape(b * GDN_HEADS, GDN_DK, GDN_DV),
        [per_head(t) for t in (qn_s, kn_s, v_s, ege_s, be_s, zs_s)], gdn_norm[l].reshape(1, 1, GDN_DV))
    og_s = og_s.reshape(b, GDN_V_W)
    tabs_s = _rope_tables(jnp.full((1,), past_len, I32))
    qlat_s, qfull_s, ckv_s, kpe_s = _mla_dec_proj(qa_s, lat_s, rope_s, tabs_s, row(mla_q_norm[l]),
                                                   row(mla_kv_norm[l]), wq, wuk)
    qpe_s = qfull_s.reshape(b, MLA_HEADS, LANE)[:, :, ROPE_LANE0:ROPE_LANE0 + MLA_ROPE]
    kpe32_s = kpe_s[:, ROPE_LANE0:ROPE_LANE0 + MLA_ROPE]
    olat_s = _paged_attn(page_table, qlat_s.reshape(b, MLA_HEADS, MLA_KV_LORA), qpe_s,
                         ckv_s.reshape(b, 1, MLA_KV_LORA), kpe32_s.reshape(b, 1, MLA_ROPE),
                         cache_kv_latent[l], cache_k_rope[l])
    x1_s, qmem_s = _post_a_sample(xs, og_s, olat_s.reshape(b, MLA_HEADS * MLA_KV_LORA), sga_s, sgb_s, wuv,
                                  gdn_w_up[l], wmu32, w_out[l], row(norm_mem[l]), w_mq[l])
    omem_s = _mem_dec(qmem_s.reshape(b, 1, MEM_W), cache_mem_k[l].reshape(b, MEM_TOKENS, MEM_W),
                      cache_mem_v[l].reshape(b, MEM_TOKENS, MEM_W))
    x2_s, h3_s, idx_s, wts_s, rank_s, cnt = _post_b_sample(x1_s, omem_s.reshape(b, MEM_W), w_mo[l],
                                                           row(norm_ffn[l]), rw, rb, cnt_p)

    counts = cnt[0, :N_EXPERTS].astype(I32)
    offs, tile_e, tile_nv, n_tiles = _route_plan(counts, (n + b) * TOP_K)
    pos_p = (offs[idx_p[:, :TOP_K]] + rank_p[:, :TOP_K]).reshape(-1)
    pos_s = (offs[idx_s[:, :TOP_K]] + rank_s[:, :TOP_K]).reshape(-1)
    xsort = jnp.zeros((n_tiles * MOE_TE, D_MODEL), F32)
    xsort = _dispatch(pos_p, h3_p, xsort, MOE_TM)
    xsort = _dispatch(pos_s, h3_s, xsort, b)
    ysort = _ffn(tile_e, tile_nv, xsort, w_gate[l], b_gate[l], w_up[l], b_up[l], w_down[l], b_down[l], n_tiles)
    y_p = _combine(pos_p, x2_p, wts_p, row(norm_final), ysort, MOE_TM)
    y_s = _combine(pos_s, x2_s, wts_s, row(norm_final), ysort, b)

    stack = lambda t: t[None]
    return (y_p.reshape(1, n, D_MODEL), y_s.reshape(b, 1, D_MODEL),
            stack(ckv_p.reshape(1, n, MLA_KV_LORA)),
            stack(kpe_p[:, ROPE_LANE0:ROPE_LANE0 + MLA_ROPE].reshape(1, n, MLA_ROPE)),
            stack(conv_p.reshape(1, GDN_CONV - 1, GDN_CONV_W)),
            stack(ssm_p.reshape(1, GDN_HEADS, GDN_DK, GDN_DV)),
            stack(mk_p.reshape(1, MEM_TOKENS, MEM_HEADS, MEM_HEAD_DIM)),
            stack(mv_p.reshape(1, MEM_TOKENS, MEM_HEADS, MEM_HEAD_DIM)),
            stack(ckv_s.reshape(b, 1, MLA_KV_LORA)),
            stack(kpe32_s.reshape(b, 1, MLA_ROPE)),
            stack(conv_s),
            stack(ssm_s.reshape(b, GDN_HEADS, GDN_DK, GDN_DV)))
```

```python
import functools
import math

import numpy as np
import jax
import jax.numpy as jnp
from jax import lax
from jax.experimental import pallas as pl
from jax.experimental.pallas import tpu as pltpu

F32 = jnp.float32
BF16 = jnp.bfloat16
I32 = jnp.int32

D_MODEL = 1024
PAGE_SIZE = 128
GDN_HEADS = 8
GDN_DK = 64
GDN_DV = 64
GDN_CONV = 4
GDN_CHUNK = 64
MLA_HEADS = 8
MLA_Q_LORA = 384
MLA_KV_LORA = 256
MLA_NOPE = 64
MLA_ROPE = 32
MLA_V = 64
ROPE_THETA = 10000.0
MEM_TOKENS = 256
MEM_HEADS = 4
MEM_HEAD_DIM = 128
N_EXPERTS = 32
TOP_K = 4
D_FF = 1024
SWIGLU_LIMIT = 7.0
SWIGLU_ALPHA = 1.702
NORM_EPS = 1e-6

GDN_QK_W = GDN_HEADS * GDN_DK
GDN_V_W = GDN_HEADS * GDN_DV
GDN_CONV_W = 2 * GDN_QK_W + GDN_V_W
MLA_QK_HEAD = MLA_NOPE + MLA_ROPE
MLA_SCALE = MLA_QK_HEAD ** -0.5
MEM_W = MEM_HEADS * MEM_HEAD_DIM
IN_SIZES = (GDN_CONV_W, GDN_V_W, GDN_HEADS, GDN_HEADS, MLA_Q_LORA,
            MLA_KV_LORA + MLA_ROPE, D_MODEL, D_MODEL)

LANE = 128
LOG2E = 1.4426950408889634
VMEM_LIMIT = 56 * 1024 * 1024

SEG_QKV = (0, 1536)
SEG_Z = (1536, 2048)
SEG_AB = (2048, 2176)
SEG_QA = (2176, 2560)
SEG_LAT = (2560, 2816)
SEG_ROPE = (2816, 2944)
SEG_GA = (2944, 3968)
SEG_GB = (3968, 4992)
PACKED_W = 4992
ROPE_LANE0 = 64


def _cparams(sem):
    return pltpu.CompilerParams(dimension_semantics=sem, vmem_limit_bytes=VMEM_LIMIT)


def _rms(x, g):
    return x * lax.rsqrt(jnp.mean(x * x, axis=-1, keepdims=True) + NORM_EPS) * g


def _dot(a, b):
    return jnp.dot(a, b, preferred_element_type=F32)


def _dot_nt(a, b):
    return lax.dot_general(a, b, (((1,), (1,)), ((), ())), preferred_element_type=F32)


def _dot_tn(a, b):
    return lax.dot_general(a, b, (((0,), (0,)), ((), ())), preferred_element_type=F32)


def _split3(x):
    x1 = x.astype(BF16)
    r = x - x1.astype(F32)
    x2 = r.astype(BF16)
    x3 = (r - x2.astype(F32)).astype(BF16)
    return x1, x2, x3


def _dot_lhs_exact(x, m):
    x1, x2, x3 = _split3(x)
    return _dot(x1, m) + _dot(x2, m) + _dot(x3, m)


def _dot_rhs_exact(m, x):
    x1, x2, x3 = _split3(x)
    return _dot(m, x1) + _dot(m, x2) + _dot(m, x3)


def _dot_f32(a, b):
    return jnp.dot(a, b, precision=lax.Precision.HIGHEST, preferred_element_type=F32)


def _inproj_kernel(x_ref, g_ref, w_ref, qkv_ref, z_ref, ab_ref, qa_ref, lat_ref, rope_ref,
                   sga_ref, sgb_ref, *, precise):
    h = _rms(x_ref[...], g_ref[...])
    if not precise:
        h = h.astype(BF16)

    def seg(s):
        w = w_ref[:, s[0]:s[1]]
        return _dot_f32(h, w) if precise else _dot(h, w)

    qkv_ref[...] = seg(SEG_QKV)
    z_ref[...] = seg(SEG_Z)
    ab_ref[...] = seg(SEG_AB)
    qa_ref[...] = seg(SEG_QA)
    lat_ref[...] = seg(SEG_LAT)
    rope_ref[...] = seg(SEG_ROPE)
    sga_ref[...] = jax.nn.sigmoid(seg(SEG_GA)).astype(sga_ref.dtype)
    sgb_ref[...] = jax.nn.sigmoid(seg(SEG_GB)).astype(sgb_ref.dtype)


def _pack_w_in(w):
    offs = np.cumsum(IN_SIZES)[:-1].tolist()
    qkv, z, a, b, qa, kva, ga, gb = jnp.split(w, offs, axis=1)
    k = w.shape[0]
    ab = jnp.concatenate([a, b, jnp.zeros((k, LANE - 2 * GDN_HEADS), w.dtype)], axis=1)
    rope = jnp.concatenate([jnp.zeros((k, ROPE_LANE0), w.dtype), kva[:, MLA_KV_LORA:],
                            jnp.zeros((k, LANE - ROPE_LANE0 - MLA_ROPE), w.dtype)], axis=1)
    return jnp.concatenate([qkv, z, ab, qa, kva[:, :MLA_KV_LORA], rope, ga, gb], axis=1)


def _inproj(x, g, wp, tm, precise=False):
    m = x.shape[0]
    widths = [s[1] - s[0] for s in (SEG_QKV, SEG_Z, SEG_AB, SEG_QA, SEG_LAT, SEG_ROPE, SEG_GA, SEG_GB)]
    dts = [F32] * 6 + ([F32] * 2 if precise else [BF16] * 2)
    row = lambda i: (i, 0)
    const = lambda i: (0, 0)
    return pl.pallas_call(
        functools.partial(_inproj_kernel, precise=precise),
        grid=(m // tm,),
        in_specs=[pl.BlockSpec((tm, D_MODEL), row), pl.BlockSpec((1, D_MODEL), const),
                  pl.BlockSpec((D_MODEL, PACKED_W), const)],
        out_specs=[pl.BlockSpec((tm, w), row) for w in widths],
        out_shape=[jax.ShapeDtypeStruct((m, w), dt) for w, dt in zip(widths, dts)],
        compiler_params=_cparams(("arbitrary",)),
        name="inproj_sample" if precise else "inproj",
    )(x, g, wp)


GDN_TB = 256
GDN_NC = GDN_TB // GDN_CHUNK
GDN_PAIRS = GDN_HEADS // 2
TAIL = 8


def _gdn_tables(tb):
    c = GDN_CHUNK
    bd = np.kron(np.eye(GDN_HEADS), np.ones((GDN_DK, GDN_DK)))
    r = np.arange(tb)
    lblk = ((r[:, None] >= r[None, :]) & (r[:, None] // c == r[None, :] // c)).astype(np.float32)
    eg = np.zeros((LANE, GDN_QK_W), np.float32)
    eb = np.zeros((LANE, GDN_QK_W), np.float32)
    for h in range(GDN_HEADS):
        eg[h, h * GDN_DK:(h + 1) * GDN_DK] = 1.0
        eb[GDN_HEADS + h, h * GDN_DK:(h + 1) * GDN_DK] = 1.0
    return (jnp.asarray(bd, BF16), jnp.asarray(lblk, BF16), jnp.asarray(eg, BF16), jnp.asarray(eb, BF16))


def _softplus(x):
    return jnp.maximum(x, 0.0) + jnp.log1p(jnp.exp(-jnp.abs(x)))


def _gdn_kernel(qkv_ref, z_ref, ab_ref, cw_ref, alog_ref, dtb_ref, gn_ref, bd_ref, lblk_ref, eg_ref, eb_ref,
                o_ref, conv_ref, ssm_ref, xp_ref, s_ref):
    tb = GDN_TB
    c = GDN_CHUNK
    i = pl.program_id(0)

    @pl.when(i == 0)
    def _():
        xp_ref[0:TAIL, :] = jnp.zeros((TAIL, GDN_CONV_W), F32)
        s_ref[...] = jnp.zeros_like(s_ref)

    xp_ref[TAIL:TAIL + tb, :] = qkv_ref[...]
    y = cw_ref[0:1, :] * xp_ref[TAIL - 3:TAIL - 3 + tb, :]
    for j in range(1, GDN_CONV):
        y = y + cw_ref[j:j + 1, :] * xp_ref[TAIL - 3 + j:TAIL - 3 + j + tb, :]
    conv_ref[...] = xp_ref[TAIL + tb - (GDN_CONV - 1):TAIL + tb, :]
    xp_ref[0:TAIL, :] = xp_ref[tb:tb + TAIL, :]

    act = y * jax.nn.sigmoid(y)
    q = act[:, 0:GDN_QK_W]
    k = act[:, GDN_QK_W:2 * GDN_QK_W]
    v = act[:, 2 * GDN_QK_W:]
    bd = bd_ref[...]

    def segsum(t):
        t1 = t.astype(BF16)
        t2 = (t - t1.astype(F32)).astype(BF16)
        return _dot(t1, bd) + _dot(t2, bd)

    qn = q * lax.rsqrt(segsum(q * q) + NORM_EPS) * (GDN_DK ** -0.5)
    kn = k * lax.rsqrt(segsum(k * k) + NORM_EPS)

    ab = ab_ref[...]
    g128 = -jnp.exp(alog_ref[...]) * _softplus(ab + dtb_ref[...])
    beta128 = jax.nn.sigmoid(ab)
    gcum = _dot_rhs_exact(lblk_ref[...], g128)
    gc_e = _dot_lhs_exact(gcum, eg_ref[...])
    beta_e = _dot_lhs_exact(beta128, eb_ref[...])
    eg_e = jnp.exp(gc_e)
    kb = kn * beta_e
    vb = v * beta_e
    kbeg = kb * eg_e
    qg = qn * eg_e

    lane = lax.broadcasted_iota(I32, (c, LANE), 1)
    sub = lax.broadcasted_iota(I32, (c, LANE), 0)
    jj = jnp.where(lane >= GDN_DK, lane - GDN_DK, lane)
    m_incl = sub >= jj
    m_strict = sub > jj
    m_diag = sub == jj
    lane2 = lax.broadcasted_iota(I32, (2 * c, LANE), 1)
    sub2 = lax.broadcasted_iota(I32, (2 * c, LANE), 0)
    m_bd = (lane2 >= GDN_DK) == (sub2 >= GDN_DK)
    eye_bd = (lane2 == sub2).astype(F32)

    def to_bd(xp):
        return jnp.where(m_bd, jnp.concatenate([xp, xp], axis=0), 0.0)

    def from_bd(xb):
        return xb[0:c, :] + xb[c:2 * c, :]

    blocks = [(ci, p) for ci in range(GDN_NC) for p in range(GDN_PAIRS)]

    def sl(ci, p):
        return slice(ci * c, (ci + 1) * c), slice(p * LANE, (p + 1) * LANE)

    gcols = [gc_e[sl(*b)] for b in blocks]
    decs = []
    for gcol in gcols:
        grow = jnp.sum(jnp.where(m_diag, gcol, 0.0), axis=0, keepdims=True)
        decs.append(jnp.where(m_incl, jnp.exp(jnp.where(m_incl, gcol - grow, 0.0)), 0.0))
    aqs = [_dot_nt(jnp.concatenate([kb[sl(*b)], qn[sl(*b)]], axis=0).astype(BF16),
                   to_bd(kn[sl(*b)]).astype(BF16)) for b in blocks]
    qks = [aq[c:2 * c, :] * dec for aq, dec in zip(aqs, decs)]
    a_bds = [to_bd(jnp.where(m_strict, aq[0:c, :] * dec, 0.0)) for aq, dec in zip(aqs, decs)]
    t_bds = [eye_bd - a for a in a_bds]
    pws = [a.astype(BF16) for a in a_bds]
    for _ in range(5):
        pws = [_dot(pw, pw).astype(BF16) for pw in pws]
        t_bds = [t + _dot(t.astype(BF16), pw) for t, pw in zip(t_bds, pws)]
    uws = [_dot(t.astype(BF16), jnp.concatenate([to_bd(vb[sl(*b)]), to_bd(kbeg[sl(*b)])], axis=1).astype(BF16))
           for t, b in zip(t_bds, blocks)]
    us = [from_bd(uw[:, 0:LANE]) for uw in uws]
    ws_ = [from_bd(uw[:, LANE:2 * LANE]) for uw in uws]

    states = [s_ref[p] for p in range(GDN_PAIRS)]
    o_parts = []
    for ci in range(GDN_NC):
        idx = [ci * GDN_PAIRS + p for p in range(GDN_PAIRS)]
        wss = [_dot(jnp.concatenate([ws_[i], qg[sl(*blocks[i])]], axis=0).astype(BF16), states[p].astype(BF16))
               for p, i in enumerate(idx)]
        vns = [us[i] - wss[p][0:c, :] for p, i in enumerate(idx)]
        o_pairs = [wss[p][c:2 * c, :] + _dot(qks[i].astype(BF16), to_bd(vns[p]).astype(BF16))
                   for p, i in enumerate(idx)]
        for p, i in enumerate(idx):
            rs, ls = sl(*blocks[i])
            glast = gc_e[ci * c + c - 1:ci * c + c, ls]
            kg = (kn[rs, ls] * jnp.exp(glast - gcols[i])).astype(BF16)
            upd = _dot_tn(kg, vns[p].astype(BF16))
            states[p] = states[p] * jnp.exp(glast) + jnp.where(m_bd, upd, 0.0)
        o_parts.append(jnp.concatenate(o_pairs, axis=1))
    o = jnp.concatenate(o_parts, axis=0)
    ms = segsum(o * o) * (1.0 / GDN_DV)
    z = z_ref[...]
    o_ref[...] = (o * lax.rsqrt(ms + NORM_EPS) * gn_ref[...] * (z * jax.nn.sigmoid(z))).astype(o_ref.dtype)
    for p in range(GDN_PAIRS):
        s_ref[p] = states[p]
        ssm_ref[p] = from_bd(states[p])


def _gdn_prompt(qkv, z, ab, conv_w, a_log, dt_bias, gnorm):
    n = qkv.shape[0]
    tb = GDN_TB
    bd, lblk, eg, eb = _gdn_tables(tb)
    pad8 = lambda t: jnp.concatenate([t.reshape(1, -1), jnp.zeros((1, LANE - t.size), F32)], axis=1)
    alog128 = pad8(a_log)
    dtb128 = pad8(dt_bias)
    gn = jnp.tile(gnorm.reshape(1, GDN_DV), (1, GDN_HEADS))
    row = lambda i: (i, 0)
    const = lambda i: (0, 0)
    o, conv, ssm = pl.pallas_call(
        _gdn_kernel,
        grid=(n // tb,),
        in_specs=[pl.BlockSpec((tb, GDN_CONV_W), row), pl.BlockSpec((tb, GDN_V_W), row),
                  pl.BlockSpec((tb, LANE), row), pl.BlockSpec((GDN_CONV, GDN_CONV_W), const),
                  pl.BlockSpec((1, LANE), const), pl.BlockSpec((1, LANE), const),
                  pl.BlockSpec((1, GDN_V_W), const), pl.BlockSpec((GDN_QK_W, GDN_QK_W), const),
                  pl.BlockSpec((tb, tb), const), pl.BlockSpec((LANE, GDN_QK_W), const),
                  pl.BlockSpec((LANE, GDN_QK_W), const)],
        out_specs=[pl.BlockSpec((tb, GDN_V_W), row),
                   pl.BlockSpec((GDN_CONV - 1, GDN_CONV_W), const),
                   pl.BlockSpec((GDN_PAIRS, GDN_DK, LANE), lambda i: (0, 0, 0))],
        out_shape=[jax.ShapeDtypeStruct((n, GDN_V_W), BF16),
                   jax.ShapeDtypeStruct((GDN_CONV - 1, GDN_CONV_W), F32),
                   jax.ShapeDtypeStruct((GDN_PAIRS, GDN_DK, LANE), F32)],
        scratch_shapes=[pltpu.VMEM((tb + TAIL, GDN_CONV_W), F32),
                        pltpu.VMEM((GDN_PAIRS, LANE, LANE), F32)],
        compiler_params=_cparams(("arbitrary",)),
        name="gdn_prompt",
    )(qkv, z, ab, conv_w, alog128, dtb128, gn, bd, lblk, eg, eb)
    ssm = ssm.reshape(GDN_PAIRS, GDN_DK, 2, GDN_DV).transpose(0, 2, 1, 3).reshape(GDN_HEADS, GDN_DK, GDN_DV)
    return o, conv, ssm


def _rope_tables(pos):
    half = MLA_ROPE // 2
    inv = 1.0 / (ROPE_THETA ** (jnp.arange(half, dtype=F32) / half))
    ang = pos.astype(F32)[:, None] * inv[None, :]
    cos, sin = jnp.cos(ang), jnp.sin(ang)
    n = pos.shape[0]
    one = jnp.ones((n, ROPE_LANE0), F32)
    zero = lambda w: jnp.zeros((n, w), F32)
    c = jnp.concatenate([one, cos, cos, jnp.ones((n, LANE - ROPE_LANE0 - MLA_ROPE), F32)], axis=1)
    sa = jnp.concatenate([zero(ROPE_LANE0), -sin, zero(LANE - ROPE_LANE0 - half)], axis=1)
    sb = jnp.concatenate([zero(ROPE_LANE0 + half), sin, zero(LANE - ROPE_LANE0 - MLA_ROPE)], axis=1)
    return c, sa, sb


def _rope_apply(x, c, sa, sb):
    w = x.shape[1]
    half = MLA_ROPE // 2
    return x * c + pltpu.roll(x, w - half, 1) * sa + pltpu.roll(x, half, 1) * sb


def _tile_heads(t, n):
    return jnp.concatenate([t] * n, axis=1)


def _pack_mla_weights(w_qb, w_kvb):
    kq = w_qb.shape[0]
    wq = w_qb.reshape(kq, MLA_HEADS, MLA_QK_HEAD)
    wq = jnp.concatenate([wq, jnp.zeros((kq, MLA_HEADS, LANE - MLA_QK_HEAD), w_qb.dtype)], axis=2)
    wq = wq.reshape(kq, MLA_HEADS * LANE)
    kk = w_kvb.shape[0]
    wkv = w_kvb.reshape(kk, MLA_HEADS, MLA_NOPE + MLA_V)
    zpad = jnp.zeros((kk, MLA_HEADS, LANE - MLA_NOPE), w_kvb.dtype)
    wk = jnp.concatenate([wkv[..., :MLA_NOPE], zpad], axis=2).reshape(kk, MLA_HEADS * LANE)
    wv = jnp.concatenate([wkv[..., MLA_NOPE:], zpad], axis=2).reshape(kk, MLA_HEADS * LANE)
    wuk = jnp.transpose(wkv[..., :MLA_NOPE], (1, 2, 0))
    wuk = jnp.concatenate([wuk, jnp.zeros((MLA_HEADS, LANE - MLA_NOPE, kk), w_kvb.dtype)], axis=1)
    wuv = jnp.transpose(wkv[..., MLA_NOPE:], (1, 0, 2))
    wuv = jnp.concatenate([wuv, jnp.zeros((MLA_HEADS, kk, LANE - MLA_V), w_kvb.dtype)], axis=2)
    return wq, wk, wv, wuk, wuv


def _mla_proj_kernel(qa_ref, lat_ref, rope_ref, c_ref, sa_ref, sb_ref, qn_ref, kvn_ref, wq_ref, wk_ref, wv_ref,
                     q_ref, k_ref, v_ref, ckv_ref, kpe_ref):
    c, sa, sb = c_ref[...], sa_ref[...], sb_ref[...]
    qa = _rms(qa_ref[...], qn_ref[...]).astype(BF16)
    q = _dot(qa, wq_ref[...])
    q = _rope_apply(q, _tile_heads(c, MLA_HEADS), _tile_heads(sa, MLA_HEADS), _tile_heads(sb, MLA_HEADS))
    q_ref[...] = (q * (MLA_SCALE * LOG2E)).astype(BF16)
    ckv = _rms(lat_ref[...], kvn_ref[...])
    ckv_ref[...] = ckv
    kpe = _rope_apply(rope_ref[...], c, sa, sb)
    kpe_ref[...] = kpe
    cb = ckv.astype(BF16)
    k_ref[...] = (_dot(cb, wk_ref[...]) + _tile_heads(kpe, MLA_HEADS)).astype(BF16)
    v_ref[...] = _dot(cb, wv_ref[...]).astype(BF16)


def _mla_proj(qa, lat, rope, tabs, q_norm, kv_norm, wq, wk, wv, tm):
    n = qa.shape[0]
    hw = MLA_HEADS * LANE
    row = lambda i: (i, 0)
    const = lambda i: (0, 0)
    return pl.pallas_call(
        _mla_proj_kernel,
        grid=(n // tm,),
        in_specs=[pl.BlockSpec((tm, MLA_Q_LORA), row), pl.BlockSpec((tm, MLA_KV_LORA), row),
                  pl.BlockSpec((tm, LANE), row), pl.BlockSpec((tm, LANE), row), pl.BlockSpec((tm, LANE), row),
                  pl.BlockSpec((tm, LANE), row), pl.BlockSpec((1, MLA_Q_LORA), const),
                  pl.BlockSpec((1, MLA_KV_LORA), const), pl.BlockSpec((MLA_Q_LORA, hw), const),
                  pl.BlockSpec((MLA_KV_LORA, hw), const), pl.BlockSpec((MLA_KV_LORA, hw), const)],
        out_specs=[pl.BlockSpec((tm, hw), row), pl.BlockSpec((tm, hw), row), pl.BlockSpec((tm, hw), row),
                   pl.BlockSpec((tm, MLA_KV_LORA), row), pl.BlockSpec((tm, LANE), row)],
        out_shape=[jax.ShapeDtypeStruct((n, hw), BF16), jax.ShapeDtypeStruct((n, hw), BF16),
                   jax.ShapeDtypeStruct((n, hw), BF16), jax.ShapeDtypeStruct((n, MLA_KV_LORA), F32),
                   jax.ShapeDtypeStruct((n, LANE), F32)],
        compiler_params=_cparams(("arbitrary",)),
        name="mla_proj",
    )(qa, lat, rope, *tabs, q_norm, kv_norm, wq, wk, wv)


ATT_TQ = 1024
ATT_TK = 512
ATT_RC = 256
NEG_BIG = -1e30


def _attn_kernel(qi_ref, kj_ref, q_ref, k_ref, v_ref, o_ref, acc_ref, m_ref, l_ref):
    step = pl.program_id(0)
    i = qi_ref[step]
    j = kj_ref[step]
    tq, tk = ATT_TQ, ATT_TK
    delta = i * tq - j * tk
    last_j = (i * tq + tq - 1) // tk

    @pl.when(j == 0)
    def _():
        acc_ref[...] = jnp.zeros_like(acc_ref)
        m_ref[...] = jnp.full_like(m_ref, NEG_BIG)
        l_ref[...] = jnp.zeros_like(l_ref)

    rc = ATT_RC
    nlc = tk // LANE

    def update(masked):
        if masked:
            diff = lax.broadcasted_iota(I32, (rc, LANE), 1) - lax.broadcasted_iota(I32, (rc, LANE), 0)
        rss = [slice(r * rc, (r + 1) * rc) for r in range(tq // rc)]

        def scores(h):
            hs = slice(h * LANE, (h + 1) * LANE)
            kh = k_ref[:, hs]
            return [_dot_nt(q_ref[rs, hs], kh) for rs in rss]

        ss_next = scores(0)
        for h in range(MLA_HEADS):
            hs = slice(h * LANE, (h + 1) * LANE)
            ss = ss_next
            if h + 1 < MLA_HEADS:
                ss_next = scores(h + 1)
            vh = v_ref[:, hs]
            scs = [[s[:, c * LANE:(c + 1) * LANE] for c in range(nlc)] for s in ss]
            if masked:
                scs = [[jnp.where(diff <= delta + (r * rc - c * LANE), sc[c], NEG_BIG) for c in range(nlc)]
                       for r, sc in enumerate(scs)]
            m_prevs = [m_ref[h, rs] for rs in rss]
            m_news = []
            for sc, m_prev in zip(scs, m_prevs):
                m_lane = sc[0]
                for c in range(1, nlc):
                    m_lane = jnp.maximum(m_lane, sc[c])
                m_news.append(jnp.maximum(m_prev, jnp.max(m_lane, axis=1, keepdims=True)))
            alphas = [jnp.exp2(m_prev - m_new) for m_prev, m_new in zip(m_prevs, m_news)]
            pcs = [[jnp.exp2(t - m_new) for t in sc] for sc, m_new in zip(scs, m_news)]
            for rs, pc, alpha, m_new in zip(rss, pcs, alphas, m_news):
                p_lane = pc[0]
                for c in range(1, nlc):
                    p_lane = p_lane + pc[c]
                l_ref[h, rs] = alpha * l_ref[h, rs] + p_lane
                m_ref[h, rs] = m_new
            pvs = [_dot(jnp.concatenate([t.astype(BF16) for t in pc], axis=1), vh) for pc in pcs]
            for rs, alpha, pv in zip(rss, alphas, pvs):
                acc_ref[rs, hs] = alpha * acc_ref[rs, hs] + pv

    fully_visible = (j * tk + tk - 1) <= i * tq

    @pl.when(fully_visible)
    def _():
        update(False)

    @pl.when(jnp.logical_not(fully_visible))
    def _():
        update(True)

    @pl.when(j == last_j)
    def _():
        for h in range(MLA_HEADS):
            hs = slice(h * LANE, (h + 1) * LANE)
            l_row = jnp.sum(l_ref[h], axis=1, keepdims=True)
            o_ref[:, hs] = (acc_ref[:, hs] / l_row).astype(o_ref.dtype)


def _attn_prompt(q, k, v):
    n = q.shape[0]
    tq, tk = ATT_TQ, ATT_TK
    nq = n // tq
    nk_of = [(i * tq + tq - 1) // tk + 1 for i in range(nq)]
    qi = np.concatenate([np.full(nk_of[i], i) for i in range(nq)]).astype(np.int32)
    kj = np.concatenate([np.arange(nk_of[i]) for i in range(nq)]).astype(np.int32)
    hw = MLA_HEADS * LANE
    grid_spec = pltpu.PrefetchScalarGridSpec(
        num_scalar_prefetch=2,
        grid=(len(qi),),
        in_specs=[pl.BlockSpec((tq, hw), lambda s, qi, kj: (qi[s], 0)),
                  pl.BlockSpec((tk, hw), lambda s, qi, kj: (kj[s], 0)),
                  pl.BlockSpec((tk, hw), lambda s, qi, kj: (kj[s], 0))],
        out_specs=pl.BlockSpec((tq, hw), lambda s, qi, kj: (qi[s], 0)),
        scratch_shapes=[pltpu.VMEM((tq, hw), F32), pltpu.VMEM((MLA_HEADS, tq, LANE), F32),
                        pltpu.VMEM((MLA_HEADS, tq, LANE), F32)],
    )
    return pl.pallas_call(
        _attn_kernel,
        grid_spec=grid_spec,
        out_shape=jax.ShapeDtypeStruct((n, hw), BF16),
        compiler_params=_cparams(("arbitrary",)),
        name="mla_attn_prompt",
    )(jnp.asarray(qi), jnp.asarray(kj), q, k, v)


def _memkv_kernel(mem_ref, g_ref, wk_ref, wv_ref, k_ref, v_ref):
    m = _rms(mem_ref[...], g_ref[...]).astype(BF16)
    k_ref[...] = _dot(m, wk_ref[...])
    v_ref[...] = _dot(m, wv_ref[...])


def _memkv(mem, g, wk, wv):
    m = mem.shape[0]
    return pl.pallas_call(
        _memkv_kernel,
        out_shape=[jax.ShapeDtypeStruct((m, MEM_W), F32), jax.ShapeDtypeStruct((m, MEM_W), F32)],
        compiler_params=pltpu.CompilerParams(vmem_limit_bytes=VMEM_LIMIT),
        name="mem_kv",
    )(mem, g, wk, wv)


POST_TM = 256


def _merge_x1(x, og, om, sga, sgb, wgu, wmu, wout):
    ga = sga.astype(F32) * _dot(og, wgu)
    gb = sgb.astype(F32) * _dot(om, wmu)
    return x + _dot((ga + gb).astype(BF16), wout)


def _mem_attend_shared(qb, mk, mv):
    outs = []
    for h in range(MEM_HEADS):
        hs = slice(h * MEM_HEAD_DIM, (h + 1) * MEM_HEAD_DIM)
        s = _dot_nt(qb[:, hs], mk[:, hs])
        p = jnp.exp2(s - jnp.max(s, axis=1, keepdims=True))
        l = jnp.sum(p, axis=1, keepdims=True)
        outs.append(_dot(p.astype(BF16), mv[:, hs]) / l)
    return jnp.concatenate(outs, axis=1)


def _router(h3, rw, rb, base, lstrict):
    tm = h3.shape[0]
    logits = jnp.dot(h3, rw, precision=lax.Precision.HIGHEST, preferred_element_type=F32) + rb
    lane = lax.broadcasted_iota(I32, (tm, LANE), 1)
    lane_f = lane.astype(F32)
    work = jnp.where(lane < N_EXPERTS, logits, -jnp.inf)
    vals, ohs, idxs = [], [], []
    for _ in range(TOP_K):
        mx = jnp.max(work, axis=1, keepdims=True)
        idx = jnp.min(jnp.where(work == mx, lane_f, float(LANE)), axis=1, keepdims=True)
        oh = lane_f == idx
        work = jnp.where(oh, -jnp.inf, work)
        vals.append(mx)
        idxs.append(idx)
        ohs.append(oh)
    es = [jnp.exp(v - vals[0]) for v in vals]
    denom = es[0] + es[1] + es[2] + es[3]
    sel = jnp.zeros((tm, LANE), F32)
    for oh in ohs:
        sel = sel + oh.astype(F32)
    rank_te = _dot(lstrict, sel.astype(BF16)) + base
    idx_out = jnp.zeros((tm, LANE), F32)
    w_out = jnp.zeros((tm, LANE), F32)
    r_out = jnp.zeros((tm, LANE), F32)
    for k in range(TOP_K):
        rk = jnp.sum(jnp.where(ohs[k], rank_te, 0.0), axis=1, keepdims=True)
        idx_out = jnp.where(lane == k, idxs[k], idx_out)
        w_out = jnp.where(lane == k, es[k] / denom, w_out)
        r_out = jnp.where(lane == k, rk, r_out)
    new_base = base + jnp.sum(sel, axis=0, keepdims=True)
    return idx_out.astype(I32), w_out, r_out.astype(I32), new_base


def _post_prompt_kernel(x_ref, og_ref, om_ref, sga_ref, sgb_ref, wgu_ref, wmu_ref, wout_ref, nmem_ref, wmq_ref,
                        mk_ref, mv_ref, wmo_ref, nffn_ref, rw_ref, rb_ref, base0_ref, ls_ref,
                        x2_ref, h3_ref, idx_ref, wts_ref, rank_ref, cnt_ref, base_ref):
    @pl.when(pl.program_id(0) == 0)
    def _():
        base_ref[...] = base0_ref[...]

    x1 = _merge_x1(x_ref[...], og_ref[...], om_ref[...], sga_ref[...], sgb_ref[...],
                   wgu_ref[...], wmu_ref[...], wout_ref[...])
    h2 = _rms(x1, nmem_ref[...]).astype(BF16)
    qb = (_dot(h2, wmq_ref[...]) * (MEM_HEAD_DIM ** -0.5 * LOG2E)).astype(BF16)
    om = _mem_attend_shared(qb, mk_ref[...], mv_ref[...])
    x2 = x1 + _dot(om.astype(BF16), wmo_ref[...])
    x2_ref[...] = x2
    h3 = _rms(x2, nffn_ref[...])
    h3_ref[...] = h3
    idx, wts, rank, nb = _router(h3, rw_ref[...], rb_ref[...], base_ref[...], ls_ref[...])
    idx_ref[...] = idx
    wts_ref[...] = wts
    rank_ref[...] = rank
    base_ref[...] = nb
    cnt_ref[...] = nb


def _strict_lower(tm):
    r = np.arange(tm)
    return jnp.asarray((r[:, None] > r[None, :]).astype(np.float32), BF16)


def _post_prompt(x, og, om, sga, sgb, wgu, wmu, wout, nmem, wmq, mk, mv, wmo, nffn, rw, rb, base0):
    n = x.shape[0]
    tm = POST_TM
    row = lambda i: (i, 0)
    const = lambda i: (0, 0)
    full = lambda a: pl.BlockSpec(a.shape, const)
    ls = _strict_lower(tm)
    ins = [x, og, om, sga, sgb, wgu, wmu, wout, nmem, wmq, mk, mv, wmo, nffn, rw, rb, base0, ls]
    in_specs = [pl.BlockSpec((tm, a.shape[1]), row) for a in ins[:5]] + [full(a) for a in ins[5:]]
    return pl.pallas_call(
        _post_prompt_kernel,
        grid=(n // tm,),
        in_specs=in_specs,
        out_specs=[pl.BlockSpec((tm, D_MODEL), row), pl.BlockSpec((tm, D_MODEL), row),
                   pl.BlockSpec((tm, LANE), row), pl.BlockSpec((tm, LANE), row), pl.BlockSpec((tm, LANE), row),
                   pl.BlockSpec((1, LANE), const)],
        out_shape=[jax.ShapeDtypeStruct((n, D_MODEL), F32), jax.ShapeDtypeStruct((n, D_MODEL), F32),
                   jax.ShapeDtypeStruct((n, LANE), I32), jax.ShapeDtypeStruct((n, LANE), F32),
                   jax.ShapeDtypeStruct((n, LANE), I32), jax.ShapeDtypeStruct((1, LANE), F32)],
        scratch_shapes=[pltpu.VMEM((1, LANE), F32)],
        compiler_params=_cparams(("arbitrary",)),
        name="post_prompt",
    )(*ins)


MOE_TE = 256
MOE_TM = 256


def _route_plan(counts, n_rows):
    te = MOE_TE
    padded = ((counts + te - 1) // te) * te
    ends = jnp.cumsum(padded)
    offs = ends - padded
    n_tiles = n_rows // te + N_EXPERTS
    tile_start = jnp.arange(n_tiles, dtype=I32) * te
    tile_e = jnp.minimum(jnp.sum((tile_start[:, None] >= ends[None, :]).astype(I32), axis=1), N_EXPERTS - 1)
    tile_nv = jnp.clip(offs[tile_e] + counts[tile_e] - tile_start, 0, te).astype(I32)
    return offs.astype(I32), tile_e, tile_nv, n_tiles


def _dispatch_kernel(pos_ref, h_ref, xs_in_ref, xs_ref, sem):
    del xs_in_ref
    tm = h_ref.shape[0]

    def body(r, carry):
        for k in range(TOP_K):
            p = pos_ref[TOP_K * r + k]
            pltpu.make_async_copy(h_ref.at[pl.ds(r, 1)], xs_ref.at[pl.ds(p, 1)], sem).start()
        return carry

    lax.fori_loop(0, tm, body, 0)
    for _ in range(TOP_K):
        pltpu.make_async_copy(h_ref, xs_ref.at[pl.ds(0, tm)], sem).wait()


def _dispatch(pos, h3, xs, tm):
    n = h3.shape[0]
    return pl.pallas_call(
        _dispatch_kernel,
        grid=(n // tm,),
        in_specs=[pl.BlockSpec((tm * TOP_K,), lambda i: (i,), memory_space=pltpu.SMEM),
                  pl.BlockSpec((tm, D_MODEL), lambda i: (i, 0)),
                  pl.BlockSpec(memory_space=pl.ANY)],
        out_specs=pl.BlockSpec(memory_space=pl.ANY),
        out_shape=jax.ShapeDtypeStruct(xs.shape, xs.dtype),
        scratch_shapes=[pltpu.SemaphoreType.DMA],
        input_output_aliases={2: 0},
        compiler_params=_cparams(("arbitrary",)),
        name="moe_dispatch",
    )(pos, h3, xs)


def _ffn_kernel(te_ref, nv_ref, x_ref, wg_ref, bg_ref, wu_ref, bu_ref, wd_ref, bd_ref, y_ref,
                wgb_ref, wub_ref, wdb_ref):
    t = pl.program_id(0)
    e = te_ref[t]
    nv = nv_ref[t]
    changed = jnp.logical_or(t == 0, te_ref[jnp.maximum(t - 1, 0)] != e)

    @pl.when(changed)
    def _():
        wgb_ref[...] = wg_ref[...].astype(BF16)
        wub_ref[...] = wu_ref[...].astype(BF16)
        wdb_ref[...] = wd_ref[...].astype(BF16)

    @pl.when(nv > 0)
    def _():
        rows = lax.broadcasted_iota(I32, x_ref.shape, 0)
        xb = jnp.where(rows < nv, x_ref[...], 0.0).astype(BF16)
        gate = jnp.minimum(_dot(xb, wgb_ref[...]) + bg_ref[...], SWIGLU_LIMIT)
        up = jnp.clip(_dot(xb, wub_ref[...]) + bu_ref[...], -SWIGLU_LIMIT, SWIGLU_LIMIT)
        act = (up + 1.0) * gate * jax.nn.sigmoid(SWIGLU_ALPHA * gate)
        y_ref[...] = _dot(act.astype(BF16), wdb_ref[...]) + bd_ref[...]

    @pl.when(nv == 0)
    def _():
        y_ref[...] = jnp.zeros_like(y_ref)


def _ffn(tile_e, tile_nv, xs, w_gate, b_gate, w_up, b_up, w_down, b_down, n_tiles):
    te = MOE_TE
    wspec = lambda: pl.BlockSpec((None, D_MODEL, D_FF), lambda t, te_, nv_: (te_[t], 0, 0))
    bspec = lambda: pl.BlockSpec((None, 1, D_FF), lambda t, te_, nv_: (te_[t], 0, 0))
    grid_spec = pltpu.PrefetchScalarGridSpec(
        num_scalar_prefetch=2,
        grid=(n_tiles,),
        in_specs=[pl.BlockSpec((te, D_MODEL), lambda t, te_, nv_: (t, 0)),
                  wspec(), bspec(), wspec(), bspec(), wspec(), bspec()],
        out_specs=pl.BlockSpec((te, D_MODEL), lambda t, te_, nv_: (t, 0)),
        scratch_shapes=[pltpu.VMEM((D_MODEL, D_FF), BF16), pltpu.VMEM((D_MODEL, D_FF), BF16),
                        pltpu.VMEM((D_FF, D_MODEL), BF16)],
    )
    return pl.pallas_call(
        _ffn_kernel,
        grid_spec=grid_spec,
        out_shape=jax.ShapeDtypeStruct(xs.shape, F32),
        compiler_params=_cparams(("arbitrary",)),
        name="moe_ffn",
    )(tile_e, tile_nv, xs, w_gate, b_gate.reshape(N_EXPERTS, 1, D_FF), w_up, b_up.reshape(N_EXPERTS, 1, D_FF),
      w_down, b_down.reshape(N_EXPERTS, 1, D_MODEL))


def _combine_kernel(pos_ref, x2_ref, w_ref, nf_ref, ys_ref, y_ref, buf_ref, sem):
    tm = x2_ref.shape[0]

    def body(r, carry):
        for k in range(TOP_K):
            p = pos_ref[TOP_K * r + k]
            pltpu.make_async_copy(ys_ref.at[pl.ds(p, 1)], buf_ref.at[k, pl.ds(r, 1)], sem).start()
        return carry

    lax.fori_loop(0, tm, body, 0)
    for k in range(TOP_K):
        pltpu.make_async_copy(ys_ref.at[pl.ds(0, tm)], buf_ref.at[k], sem).wait()
    w = w_ref[...]
    y = x2_ref[...]
    for k in range(TOP_K):
        y = y + w[:, k:k + 1] * buf_ref[k]
    y_ref[...] = _rms(y, nf_ref[...])


def _combine(pos, x2, wts, nf, ys, tm):
    n = x2.shape[0]
    return pl.pallas_call(
        _combine_kernel,
        grid=(n // tm,),
        in_specs=[pl.BlockSpec((tm * TOP_K,), lambda i: (i,), memory_space=pltpu.SMEM),
                  pl.BlockSpec((tm, D_MODEL), lambda i: (i, 0)),
                  pl.BlockSpec((tm, LANE), lambda i: (i, 0)),
                  pl.BlockSpec((1, D_MODEL), lambda i: (0, 0)),
                  pl.BlockSpec(memory_space=pl.ANY)],
        out_specs=pl.BlockSpec((tm, D_MODEL), lambda i: (i, 0)),
        out_shape=jax.ShapeDtypeStruct((n, D_MODEL), F32),
        scratch_shapes=[pltpu.VMEM((TOP_K, tm, D_MODEL), F32), pltpu.SemaphoreType.DMA],
        compiler_params=_cparams(("arbitrary",)),
        name="moe_combine",
    )(pos, x2, wts, nf, ys)


def _gdn_dec_prep_kernel(qkv_ref, cs_ref, z_ref, ab_ref, cw_ref, alog_ref, dtb_ref, bd_ref, eg_ref, eb_ref,
                         csn_ref, q_ref, k_ref, v_ref, ege_ref, be_ref, zs_ref):
    x = qkv_ref[...]
    y = cw_ref[GDN_CONV - 1:GDN_CONV, :] * x
    for j in range(GDN_CONV - 1):
        y = y + cw_ref[j:j + 1, :] * cs_ref[:, j, :]
    for j in range(GDN_CONV - 2):
        csn_ref[:, j, :] = cs_ref[:, j + 1, :]
    csn_ref[:, GDN_CONV - 2, :] = x
    act = y * jax.nn.sigmoid(y)
    q = act[:, 0:GDN_QK_W]
    k = act[:, GDN_QK_W:2 * GDN_QK_W]
    bd = bd_ref[...]
    q_ref[...] = q * lax.rsqrt(_dot_lhs_exact(q * q, bd) + NORM_EPS) * (GDN_DK ** -0.5)
    k_ref[...] = k * lax.rsqrt(_dot_lhs_exact(k * k, bd) + NORM_EPS)
    v_ref[...] = act[:, 2 * GDN_QK_W:]
    ab = ab_ref[...]
    g128 = -jnp.exp(alog_ref[...]) * _softplus(ab + dtb_ref[...])
    ege_ref[...] = jnp.exp(_dot_lhs_exact(g128, eg_ref[...]))
    be_ref[...] = _dot_lhs_exact(jax.nn.sigmoid(ab), eb_ref[...])
    z = z_ref[...]
    zs_ref[...] = z * jax.nn.sigmoid(z)


def _gdn_dec_prep(qkv, cs, z, ab, conv_w, alog128, dtb128):
    b = qkv.shape[0]
    bd, _, eg, eb = _gdn_tables(GDN_CHUNK)
    wide = jax.ShapeDtypeStruct((b, GDN_QK_W), F32)
    return pl.pallas_call(
        _gdn_dec_prep_kernel,
        out_shape=[jax.ShapeDtypeStruct(cs.shape, F32)] + [wide] * 6,
        compiler_params=pltpu.CompilerParams(vmem_limit_bytes=VMEM_LIMIT),
        name="gdn_decode_prep",
    )(qkv, cs, z, ab, conv_w, alog128, dtb128, bd, eg, eb)


GDN_DEC_B = 64


def _gdn_dec_step_kernel(s_ref, q_ref, k_ref, v_ref, eg_ref, be_ref, zs_ref, gn_ref, sn_ref, o_ref):
    s = s_ref[...]
    q, k, v = q_ref[...], k_ref[...], v_ref[...]
    eg, be = eg_ref[...], be_ref[...]
    eye = lax.broadcasted_iota(I32, (GDN_DK, GDN_DK), 0) == lax.broadcasted_iota(I32, (GDN_DK, GDN_DK), 1)
    kcol = jnp.sum(jnp.where(eye, k, 0.0), axis=2, keepdims=True)
    qcol = jnp.sum(jnp.where(eye, q, 0.0), axis=2, keepdims=True)
    ks = jnp.sum(s * kcol, axis=1, keepdims=True)
    qs = jnp.sum(s * qcol, axis=1, keepdims=True)
    vn = be * (v - eg * ks)
    qk = jnp.sum(q * k, axis=2, keepdims=True)
    o = eg * qs + qk * vn
    sn_ref[...] = s * eg + kcol * vn
    ms = jnp.mean(o * o, axis=2, keepdims=True)
    o_ref[...] = o * lax.rsqrt(ms + NORM_EPS) * gn_ref[...] * zs_ref[...]


def _gdn_dec_step(s0, rows, gn):
    bh = s0.shape[0]
    bb = GDN_DEC_B
    sspec = pl.BlockSpec((bb, GDN_DK, GDN_DV), lambda i: (i, 0, 0))
    rspec = pl.BlockSpec((bb, 1, GDN_DV), lambda i: (i, 0, 0))
    return pl.pallas_call(
        _gdn_dec_step_kernel,
        grid=(bh // bb,),
        in_specs=[sspec] + [rspec] * 6 + [pl.BlockSpec((1, 1, GDN_DV), lambda i: (0, 0, 0))],
        out_specs=[sspec, rspec],
        out_shape=[jax.ShapeDtypeStruct(s0.shape, F32), jax.ShapeDtypeStruct((bh, 1, GDN_DV), F32)],
        compiler_params=_cparams(("arbitrary",)),
        name="gdn_decode_step",
    )(s0, *rows, gn)


def _mla_dec_proj_kernel(qa_ref, lat_ref, rope_ref, c_ref, sa_ref, sb_ref, qn_ref, kvn_ref, wq_ref, wuk_ref,
                         qlat_ref, q_ref, ckv_ref, kpe_ref):
    c, sa, sb = c_ref[...], sa_ref[...], sb_ref[...]
    q = _dot_f32(_rms(qa_ref[...], qn_ref[...]), wq_ref[...])
    q = _rope_apply(q, _tile_heads(c, MLA_HEADS), _tile_heads(sa, MLA_HEADS), _tile_heads(sb, MLA_HEADS))
    q = q * (MLA_SCALE * LOG2E)
    q_ref[...] = q
    for h in range(MLA_HEADS):
        qlat_ref[:, h * MLA_KV_LORA:(h + 1) * MLA_KV_LORA] = _dot_f32(q[:, h * LANE:(h + 1) * LANE], wuk_ref[h])
    ckv_ref[...] = _rms(lat_ref[...], kvn_ref[...])
    kpe_ref[...] = _rope_apply(rope_ref[...], c, sa, sb)


def _mla_dec_proj(qa, lat, rope, tabs, q_norm, kv_norm, wq, wuk):
    b = qa.shape[0]
    return pl.pallas_call(
        _mla_dec_proj_kernel,
        out_shape=[jax.ShapeDtypeStruct((b, MLA_HEADS * MLA_KV_LORA), F32),
                   jax.ShapeDtypeStruct((b, MLA_HEADS * LANE), F32),
                   jax.ShapeDtypeStruct((b, MLA_KV_LORA), F32), jax.ShapeDtypeStruct((b, LANE), F32)],
        compiler_params=pltpu.CompilerParams(vmem_limit_bytes=VMEM_LIMIT),
        name="mla_decode_proj",
    )(qa, lat, rope, *tabs, q_norm, kv_norm, wq, wuk)


PAGE_GROUP = 32


def _paged_attn_kernel(pt_ref, qlat_ref, qpe_ref, ckv_ref, kpe_ref, lat_hbm, rope_hbm, o_ref,
                       latbuf, ropebuf, sem):
    b = pl.program_id(0)
    nb = pl.num_programs(0)
    n_pages = pt_ref.shape[0] // nb
    ngroups = n_pages // PAGE_GROUP
    total = nb * ngroups

    def copies(gidx, slot):
        out = []
        for g in range(PAGE_GROUP):
            page = pt_ref[gidx * PAGE_GROUP + g]
            out.append(pltpu.make_async_copy(lat_hbm.at[page], latbuf.at[slot, g], sem.at[slot]))
            out.append(pltpu.make_async_copy(rope_hbm.at[page], ropebuf.at[slot, g], sem.at[slot]))
        return out

    @pl.when(b == 0)
    def _():
        for cp in copies(0, 0):
            cp.start()

    q = qlat_ref[0]
    qp = qpe_ref[0]
    nh = MLA_HEADS

    def hi_lo(x):
        hi = x.astype(BF16)
        return jnp.concatenate([hi, (x - hi.astype(F32)).astype(BF16)], axis=0)

    q2, qp2 = hi_lo(q), hi_lo(qp)

    def body(t, carry):
        m, l, acc = carry
        gidx = b * ngroups + t
        slot = lax.rem(gidx, 2)

        @pl.when(gidx + 1 < total)
        def _():
            for cp in copies(gidx + 1, 1 - slot):
                cp.start()

        for cp in copies(gidx, slot):
            cp.wait()
        lats = [latbuf[slot, g].astype(BF16) for g in range(PAGE_GROUP)]
        ropes = [ropebuf[slot, g].astype(BF16) for g in range(PAGE_GROUP)]
        ss = [_dot_nt(q2, latg) for latg in lats]
        sr = [_dot(qp2, ropeg) for ropeg in ropes]
        s2 = jnp.concatenate([a + b_ for a, b_ in zip(ss, sr)], axis=1)
        s = s2[0:nh] + s2[nh:2 * nh]
        m_new = jnp.maximum(m, jnp.max(s, axis=1, keepdims=True))
        alpha = jnp.exp2(m - m_new)
        p = jnp.exp2(s - m_new)
        l = alpha * l + jnp.sum(p, axis=1, keepdims=True)
        pb = hi_lo(p)
        pvs = [_dot(pb[:, g * PAGE_SIZE:(g + 1) * PAGE_SIZE], lats[g]) for g in range(PAGE_GROUP)]
        while len(pvs) > 1:
            pvs = [pvs[t] + pvs[t + 1] for t in range(0, len(pvs), 2)]
        pv = pvs[0]
        return m_new, l, alpha * acc + pv[0:nh] + pv[nh:2 * nh]

    init = (jnp.full((MLA_HEADS, 1), NEG_BIG, F32), jnp.zeros((MLA_HEADS, 1), F32),
            jnp.zeros((MLA_HEADS, MLA_KV_LORA), F32))
    m, l, acc = lax.fori_loop(0, ngroups, body, init)
    ck = ckv_ref[0]
    kp = kpe_ref[0]
    s_new = jnp.sum(q * ck, axis=1, keepdims=True) + jnp.sum(qp * kp, axis=1, keepdims=True)
    m_new = jnp.maximum(m, s_new)
    alpha = jnp.exp2(m - m_new)
    p_new = jnp.exp2(s_new - m_new)
    l = alpha * l + p_new
    o_ref[0] = (alpha * acc + p_new * ck) / l


def _paged_attn(page_table, qlat, qpe, ckv, kpe, pool_lat, pool_rope):
    b, n_pages = page_table.shape
    grid_spec = pltpu.PrefetchScalarGridSpec(
        num_scalar_prefetch=1,
        grid=(b,),
        in_specs=[pl.BlockSpec((1, MLA_HEADS, MLA_KV_LORA), lambda i, pt: (i, 0, 0)),
                  pl.BlockSpec((1, MLA_HEADS, MLA_ROPE), lambda i, pt: (i, 0, 0)),
                  pl.BlockSpec((1, 1, MLA_KV_LORA), lambda i, pt: (i, 0, 0)),
                  pl.BlockSpec((1, 1, MLA_ROPE), lambda i, pt: (i, 0, 0)),
                  pl.BlockSpec(memory_space=pl.ANY), pl.BlockSpec(memory_space=pl.ANY)],
        out_specs=pl.BlockSpec((1, MLA_HEADS, MLA_KV_LORA), lambda i, pt: (i, 0, 0)),
        scratch_shapes=[pltpu.VMEM((2, PAGE_GROUP, PAGE_SIZE, MLA_KV_LORA), F32),
                        pltpu.VMEM((2, PAGE_GROUP, MLA_ROPE, PAGE_SIZE), F32),
                        pltpu.SemaphoreType.DMA((2,))],
    )
    return pl.pallas_call(
        _paged_attn_kernel,
        grid_spec=grid_spec,
        out_shape=jax.ShapeDtypeStruct((b, MLA_HEADS, MLA_KV_LORA), F32),
        compiler_params=_cparams(("arbitrary",)),
        name="mla_paged_attn",
    )(page_table.reshape(-1), qlat, qpe, ckv, kpe, pool_lat, pool_rope)


MEM_DEC_B = 8


def _mem_dec_kernel(q_ref, mk_ref, mv_ref, o_ref):
    q = q_ref[...]
    outs = []
    for h in range(MEM_HEADS):
        hs = slice(h * MEM_HEAD_DIM, (h + 1) * MEM_HEAD_DIM)
        s = jnp.sum(mk_ref[:, :, hs] * q[:, :, hs], axis=2, keepdims=True)
        p = jnp.exp(s - jnp.max(s, axis=1, keepdims=True))
        l = jnp.sum(p, axis=1, keepdims=True)
        outs.append(jnp.sum(p * mv_ref[:, :, hs], axis=1, keepdims=True) / l)
    o_ref[...] = jnp.concatenate(outs, axis=2)


def _mem_dec(q, mk, mv):
    b = q.shape[0]
    bb = MEM_DEC_B
    qspec = pl.BlockSpec((bb, 1, MEM_W), lambda i: (i, 0, 0))
    kspec = pl.BlockSpec((bb, MEM_TOKENS, MEM_W), lambda i: (i, 0, 0))
    return pl.pallas_call(
        _mem_dec_kernel,
        grid=(b // bb,),
        in_specs=[qspec, kspec, kspec],
        out_specs=qspec,
        out_shape=jax.ShapeDtypeStruct((b, 1, MEM_W), F32),
        compiler_params=_cparams(("arbitrary",)),
        name="mem_attn_decode",
    )(q, mk, mv)


def _post_a_sample_kernel(x_ref, og_ref, olat_ref, sga_ref, sgb_ref, wuv_ref, wgu_ref, wmu_ref, wout_ref,
                          nmem_ref, wmq_ref, x1_ref, q_ref):
    olat = olat_ref[...]
    om = jnp.concatenate([_dot_f32(olat[:, h * MLA_KV_LORA:(h + 1) * MLA_KV_LORA], wuv_ref[h])
                          for h in range(MLA_HEADS)], axis=1)
    ga = sga_ref[...] * _dot_f32(og_ref[...], wgu_ref[...])
    gb = sgb_ref[...] * _dot_f32(om, wmu_ref[...])
    x1 = x_ref[...] + _dot_f32(ga + gb, wout_ref[...])
    x1_ref[...] = x1
    q_ref[...] = _dot_f32(_rms(x1, nmem_ref[...]), wmq_ref[...]) * (MEM_HEAD_DIM ** -0.5)


def _post_a_sample(x, og, olat, sga, sgb, wuv, wgu, wmu, wout, nmem, wmq):
    b = x.shape[0]
    return pl.pallas_call(
        _post_a_sample_kernel,
        out_shape=[jax.ShapeDtypeStruct((b, D_MODEL), F32), jax.ShapeDtypeStruct((b, MEM_W), F32)],
        compiler_params=pltpu.CompilerParams(vmem_limit_bytes=VMEM_LIMIT),
        name="post_a_sample",
    )(x, og, olat, sga, sgb, wuv, wgu, wmu, wout, nmem, wmq)


def _post_b_sample_kernel(x1_ref, om_ref, wmo_ref, nffn_ref, rw_ref, rb_ref, base0_ref, ls_ref,
                          x2_ref, h3_ref, idx_ref, wts_ref, rank_ref, cnt_ref):
    x2 = x1_ref[...] + _dot_f32(om_ref[...], wmo_ref[...])
    x2_ref[...] = x2
    h3 = _rms(x2, nffn_ref[...])
    h3_ref[...] = h3
    idx, wts, rank, nb = _router(h3, rw_ref[...], rb_ref[...], base0_ref[...], ls_ref[...])
    idx_ref[...] = idx
    wts_ref[...] = wts
    rank_ref[...] = rank
    cnt_ref[...] = nb


def _post_b_sample(x1, om, wmo, nffn, rw, rb, base0):
    b = x1.shape[0]
    return pl.pallas_call(
        _post_b_sample_kernel,
        out_shape=[jax.ShapeDtypeStruct((b, D_MODEL), F32), jax.ShapeDtypeStruct((b, D_MODEL), F32),
                   jax.ShapeDtypeStruct((b, LANE), I32), jax.ShapeDtypeStruct((b, LANE), F32),
                   jax.ShapeDtypeStruct((b, LANE), I32), jax.ShapeDtypeStruct((1, LANE), F32)],
        compiler_params=pltpu.CompilerParams(vmem_limit_bytes=VMEM_LIMIT),
        name="post_b_sample",
    )(x1, om, wmo, nffn, rw, rb, base0, _strict_lower(b))


def _pad_lanes(t, width=LANE):
    t = t.reshape(1, -1)
    return jnp.concatenate([t, jnp.zeros((1, width - t.shape[1]), t.dtype)], axis=1)


def kernel(x_prompt, x_sample, mem_prompt, cache_kv_latent, cache_k_rope, cache_mem_k, cache_mem_v,
           state_gdn_conv, state_gdn_ssm, page_table,
           norm_mix, w_in, gdn_conv_w, gdn_a_log, gdn_dt_bias, gdn_norm, gdn_w_up,
           mla_q_norm, mla_w_qb, mla_kv_norm, mla_w_kvb, mla_w_up, w_out,
           norm_mem, mem_in_norm, w_mq, w_mk, w_mv, w_mo,
           norm_ffn, router_w, router_b, w_gate, b_gate, w_up, b_up, w_down, b_down, norm_final):
    depth = w_in.shape[0]
    assert depth == 1, "single-layer trunk"
    l = 0
    n = x_prompt.shape[1]
    b = x_sample.shape[0]
    past_len = page_table.shape[1] * PAGE_SIZE
    xp = x_prompt.reshape(n, D_MODEL)
    xs = x_sample.reshape(b, D_MODEL)
    row = lambda t: t.reshape(1, -1)

    wp32 = _pack_w_in(w_in[l])
    wp16 = wp32.astype(BF16)
    wq, wk, wv, wuk, wuv = _pack_mla_weights(mla_w_qb[l], mla_w_kvb[l])
    wmu32 = mla_w_up[l].reshape(MLA_HEADS, MLA_V, D_MODEL)
    wmu32 = jnp.concatenate([wmu32, jnp.zeros((MLA_HEADS, LANE - MLA_V, D_MODEL), F32)],
                            axis=1).reshape(MLA_HEADS * LANE, D_MODEL)
    rw = jnp.concatenate([router_w[l], jnp.zeros((D_MODEL, LANE - N_EXPERTS), F32)], axis=1)
    rb = _pad_lanes(router_b[l])
    alog128 = _pad_lanes(gdn_a_log[l])
    dtb128 = _pad_lanes(gdn_dt_bias[l])

    mk_p, mv_p = _memkv(mem_prompt.reshape(MEM_TOKENS, D_MODEL), row(mem_in_norm[l]),
                        w_mk[l].astype(BF16), w_mv[l].astype(BF16))
    qkv, z, ab, qa, lat, rope, sga, sgb = _inproj(xp, row(norm_mix[l]), wp16, 256)
    og_p, conv_p, ssm_p = _gdn_prompt(qkv, z, ab, gdn_conv_w[l], gdn_a_log[l], gdn_dt_bias[l], gdn_norm[l])
    tabs_p = _rope_tables(jnp.arange(n))
    q, k, v, ckv_p, kpe_p = _mla_proj(qa, lat, rope, tabs_p, row(mla_q_norm[l]), row(mla_kv_norm[l]),
                                      wq.astype(BF16), wk.astype(BF16), wv.astype(BF16), 256)
    om_p = _attn_prompt(q, k, v)
    base0 = jnp.zeros((1, LANE), F32)
    x2_p, h3_p, idx_p, wts_p, rank_p, cnt_p = _post_prompt(
        xp, og_p, om_p, sga, sgb, gdn_w_up[l].astype(BF16), wmu32.astype(BF16), w_out[l].astype(BF16),
        row(norm_mem[l]), w_mq[l].astype(BF16), mk_p.astype(BF16), mv_p.astype(BF16), w_mo[l].astype(BF16),
        row(norm_ffn[l]), rw, rb, base0)

    qkv_s, z_s, ab_s, qa_s, lat_s, rope_s, sga_s, sgb_s = _inproj(xs, row(norm_mix[l]), wp32, b, precise=True)
    conv_s, qn_s, kn_s, v_s, ege_s, be_s, zs_s = _gdn_dec_prep(
        qkv_s, state_gdn_conv.reshape(state_gdn_conv.shape[1:]), z_s, ab_s, gdn_conv_w[l], alog128, dtb128)
    per_head = lambda t: t.reshape(b * GDN_HEADS, 1, GDN_DV)
    ssm_s, og_s = _gdn_dec_step(
        state_gdn_ssm.reshape(b * GDN_HEADS, GDN_DK, GDN_DV),
        [per_head(t) for t in (qn_s, kn_s, v_s, ege_s, be_s, zs_s)], gdn_norm[l].reshape(1, 1, GDN_DV))
    og_s = og_s.reshape(b, GDN_V_W)
    tabs_s = _rope_tables(jnp.full((1,), past_len, I32))
    qlat_s, qfull_s, ckv_s, kpe_s = _mla_dec_proj(qa_s, lat_s, rope_s, tabs_s, row(mla_q_norm[l]),
                                                   row(mla_kv_norm[l]), wq, wuk)
    qpe_s = qfull_s.reshape(b, MLA_HEADS, LANE)[:, :, ROPE_LANE0:ROPE_LANE0 + MLA_ROPE]
    kpe32_s = kpe_s[:, ROPE_LANE0:ROPE_LANE0 + MLA_ROPE]
    olat_s = _paged_attn(page_table, qlat_s.reshape(b, MLA_HEADS, MLA_KV_LORA), qpe_s,
                         ckv_s.reshape(b, 1, MLA_KV_LORA), kpe32_s.reshape(b, 1, MLA_ROPE),
                         cache_kv_latent.reshape(cache_kv_latent.shape[1:]),
                         jnp.swapaxes(cache_k_rope, 2, 3).reshape(cache_k_rope.shape[1], MLA_ROPE, PAGE_SIZE))
    x1_s, qmem_s = _post_a_sample(xs, og_s, olat_s.reshape(b, MLA_HEADS * MLA_KV_LORA), sga_s, sgb_s, wuv,
                                  gdn_w_up[l], wmu32, w_out[l], row(norm_mem[l]), w_mq[l])
    omem_s = _mem_dec(qmem_s.reshape(b, 1, MEM_W), cache_mem_k.reshape(b, MEM_TOKENS, MEM_W),
                      cache_mem_v.reshape(b, MEM_TOKENS, MEM_W))
    x2_s, h3_s, idx_s, wts_s, rank_s, cnt = _post_b_sample(x1_s, omem_s.reshape(b, MEM_W), w_mo[l],
                                                           row(norm_ffn[l]), rw, rb, cnt_p)

    counts = cnt[0, :N_EXPERTS].astype(I32)
    offs, tile_e, tile_nv, n_tiles = _route_plan(counts, (n + b) * TOP_K)
    pos_p = (offs[idx_p[:, :TOP_K]] + rank_p[:, :TOP_K]).reshape(-1)
    pos_s = (offs[idx_s[:, :TOP_K]] + rank_s[:, :TOP_K]).reshape(-1)
    xsort = jnp.zeros((n_tiles * MOE_TE, D_MODEL), F32)
    xsort = _dispatch(pos_p, h3_p, xsort, MOE_TM)
    xsort = _dispatch(pos_s, h3_s, xsort, b)
    ysort = _ffn(tile_e, tile_nv, xsort, w_gate[l], b_gate[l], w_up[l], b_up[l], w_down[l], b_down[l], n_tiles)
    y_p = _combine(pos_p, x2_p, wts_p, row(norm_final), ysort, MOE_TM)
    y_s = _combine(pos_s, x2_s, wts_s, row(norm_final), ysort, b)

    stack = lambda t: t[None]
    return (y_p.reshape(1, n, D_MODEL), y_s.reshape(b, 1, D_MODEL),
            stack(ckv_p.reshape(1, n, MLA_KV_LORA)),
            stack(kpe_p[:, ROPE_LANE0:ROPE_LANE0 + MLA_ROPE].reshape(1, n, MLA_ROPE)),
            stack(conv_p.reshape(1, GDN_CONV - 1, GDN_CONV_W)),
            stack(ssm_p.reshape(1, GDN_HEADS, GDN_DK, GDN_DV)),
            stack(mk_p.reshape(1, MEM_TOKENS, MEM_HEADS, MEM_HEAD_DIM)),
            stack(mv_p.reshape(1, MEM_TOKENS, MEM_HEADS, MEM_HEAD_DIM)),
            stack(ckv_s.reshape(b, 1, MLA_KV_LORA)),
            stack(kpe32_s.reshape(b, 1, MLA_ROPE)),
            stack(conv_s),
            stack(ssm_s.reshape(b, GDN_HEADS, GDN_DK, GDN_DV)))
```

```python
import functools
import math

import numpy as np
import jax
import jax.numpy as jnp
from jax import lax
from jax.experimental import pallas as pl
from jax.experimental.pallas import tpu as pltpu

F32 = jnp.float32
BF16 = jnp.bfloat16
I32 = jnp.int32

D_MODEL = 1024
PAGE_SIZE = 128
GDN_HEADS = 8
GDN_DK = 64
GDN_DV = 64
GDN_CONV = 4
GDN_CHUNK = 64
MLA_HEADS = 8
MLA_Q_LORA = 384
MLA_KV_LORA = 256
MLA_NOPE = 64
MLA_ROPE = 32
MLA_V = 64
ROPE_THETA = 10000.0
MEM_TOKENS = 256
MEM_HEADS = 4
MEM_HEAD_DIM = 128
N_EXPERTS = 32
TOP_K = 4
D_FF = 1024
SWIGLU_LIMIT = 7.0
SWIGLU_ALPHA = 1.702
NORM_EPS = 1e-6

GDN_QK_W = GDN_HEADS * GDN_DK
GDN_V_W = GDN_HEADS * GDN_DV
GDN_CONV_W = 2 * GDN_QK_W + GDN_V_W
MLA_QK_HEAD = MLA_NOPE + MLA_ROPE
MLA_SCALE = MLA_QK_HEAD ** -0.5
MEM_W = MEM_HEADS * MEM_HEAD_DIM
IN_SIZES = (GDN_CONV_W, GDN_V_W, GDN_HEADS, GDN_HEADS, MLA_Q_LORA,
            MLA_KV_LORA + MLA_ROPE, D_MODEL, D_MODEL)

LANE = 128
LOG2E = 1.4426950408889634
VMEM_LIMIT = 56 * 1024 * 1024

SEG_QKV = (0, 1536)
SEG_Z = (1536, 2048)
SEG_AB = (2048, 2176)
SEG_QA = (2176, 2560)
SEG_LAT = (2560, 2816)
SEG_ROPE = (2816, 2944)
SEG_GA = (2944, 3968)
SEG_GB = (3968, 4992)
PACKED_W = 4992
ROPE_LANE0 = 64


def _cparams(sem):
    return pltpu.CompilerParams(dimension_semantics=sem, vmem_limit_bytes=VMEM_LIMIT)


def _rms(x, g):
    return x * lax.rsqrt(jnp.mean(x * x, axis=-1, keepdims=True) + NORM_EPS) * g


def _dot(a, b):
    return jnp.dot(a, b, preferred_element_type=F32)


def _dot_nt(a, b):
    return lax.dot_general(a, b, (((1,), (1,)), ((), ())), preferred_element_type=F32)


def _dot_tn(a, b):
    return lax.dot_general(a, b, (((0,), (0,)), ((), ())), preferred_element_type=F32)


def _split3(x):
    x1 = x.astype(BF16)
    r = x - x1.astype(F32)
    x2 = r.astype(BF16)
    x3 = (r - x2.astype(F32)).astype(BF16)
    return x1, x2, x3


def _dot_lhs_exact(x, m):
    x1, x2, x3 = _split3(x)
    return _dot(x1, m) + _dot(x2, m) + _dot(x3, m)


def _dot_rhs_exact(m, x):
    x1, x2, x3 = _split3(x)
    return _dot(m, x1) + _dot(m, x2) + _dot(m, x3)


def _dot_f32(a, b):
    return jnp.dot(a, b, precision=lax.Precision.HIGHEST, preferred_element_type=F32)


def _inproj_kernel(x_ref, g_ref, w_ref, qkv_ref, z_ref, ab_ref, qa_ref, lat_ref, rope_ref,
                   sga_ref, sgb_ref, *, precise):
    h = _rms(x_ref[...], g_ref[...])
    if not precise:
        h = h.astype(BF16)

    def seg(s):
        w = w_ref[:, s[0]:s[1]]
        return _dot_f32(h, w) if precise else _dot(h, w)

    qkv_ref[...] = seg(SEG_QKV)
    z_ref[...] = seg(SEG_Z)
    ab_ref[...] = seg(SEG_AB)
    qa_ref[...] = seg(SEG_QA)
    lat_ref[...] = seg(SEG_LAT)
    rope_ref[...] = seg(SEG_ROPE)
    sga_ref[...] = jax.nn.sigmoid(seg(SEG_GA)).astype(sga_ref.dtype)
    sgb_ref[...] = jax.nn.sigmoid(seg(SEG_GB)).astype(sgb_ref.dtype)


def _pack_w_in(w):
    offs = np.cumsum(IN_SIZES)[:-1].tolist()
    qkv, z, a, b, qa, kva, ga, gb = jnp.split(w, offs, axis=1)
    k = w.shape[0]
    ab = jnp.concatenate([a, b, jnp.zeros((k, LANE - 2 * GDN_HEADS), w.dtype)], axis=1)
    rope = jnp.concatenate([jnp.zeros((k, ROPE_LANE0), w.dtype), kva[:, MLA_KV_LORA:],
                            jnp.zeros((k, LANE - ROPE_LANE0 - MLA_ROPE), w.dtype)], axis=1)
    return jnp.concatenate([qkv, z, ab, qa, kva[:, :MLA_KV_LORA], rope, ga, gb], axis=1)


def _inproj(x, g, wp, tm, precise=False):
    m = x.shape[0]
    widths = [s[1] - s[0] for s in (SEG_QKV, SEG_Z, SEG_AB, SEG_QA, SEG_LAT, SEG_ROPE, SEG_GA, SEG_GB)]
    dts = [F32] * 6 + ([F32] * 2 if precise else [BF16] * 2)
    row = lambda i: (i, 0)
    const = lambda i: (0, 0)
    return pl.pallas_call(
        functools.partial(_inproj_kernel, precise=precise),
        grid=(m // tm,),
        in_specs=[pl.BlockSpec((tm, D_MODEL), row), pl.BlockSpec((1, D_MODEL), const),
                  pl.BlockSpec((D_MODEL, PACKED_W), const)],
        out_specs=[pl.BlockSpec((tm, w), row) for w in widths],
        out_shape=[jax.ShapeDtypeStruct((m, w), dt) for w, dt in zip(widths, dts)],
        compiler_params=_cparams(("arbitrary",)),
        name="inproj_sample" if precise else "inproj",
    )(x, g, wp)


GDN_TB = 256
GDN_NC = GDN_TB // GDN_CHUNK
GDN_PAIRS = GDN_HEADS // 2
TAIL = 8


def _gdn_tables(tb):
    c = GDN_CHUNK
    bd = np.kron(np.eye(GDN_HEADS), np.ones((GDN_DK, GDN_DK)))
    r = np.arange(tb)
    lblk = ((r[:, None] >= r[None, :]) & (r[:, None] // c == r[None, :] // c)).astype(np.float32)
    eg = np.zeros((LANE, GDN_QK_W), np.float32)
    eb = np.zeros((LANE, GDN_QK_W), np.float32)
    for h in range(GDN_HEADS):
        eg[h, h * GDN_DK:(h + 1) * GDN_DK] = 1.0
        eb[GDN_HEADS + h, h * GDN_DK:(h + 1) * GDN_DK] = 1.0
    return (jnp.asarray(bd, BF16), jnp.asarray(lblk, BF16), jnp.asarray(eg, BF16), jnp.asarray(eb, BF16))


def _softplus(x):
    return jnp.maximum(x, 0.0) + jnp.log1p(jnp.exp(-jnp.abs(x)))


def _gdn_kernel(qkv_ref, z_ref, ab_ref, cw_ref, alog_ref, dtb_ref, gn_ref, bd_ref, lblk_ref, eg_ref, eb_ref,
                o_ref, conv_ref, ssm_ref, xp_ref, s_ref):
    tb = GDN_TB
    c = GDN_CHUNK
    i = pl.program_id(0)

    @pl.when(i == 0)
    def _():
        xp_ref[0:TAIL, :] = jnp.zeros((TAIL, GDN_CONV_W), F32)
        s_ref[...] = jnp.zeros_like(s_ref)

    xp_ref[TAIL:TAIL + tb, :] = qkv_ref[...]
    y = cw_ref[0:1, :] * xp_ref[TAIL - 3:TAIL - 3 + tb, :]
    for j in range(1, GDN_CONV):
        y = y + cw_ref[j:j + 1, :] * xp_ref[TAIL - 3 + j:TAIL - 3 + j + tb, :]
    conv_ref[...] = xp_ref[TAIL + tb - (GDN_CONV - 1):TAIL + tb, :]
    xp_ref[0:TAIL, :] = xp_ref[tb:tb + TAIL, :]

    act = y * jax.nn.sigmoid(y)
    q = act[:, 0:GDN_QK_W]
    k = act[:, GDN_QK_W:2 * GDN_QK_W]
    v = act[:, 2 * GDN_QK_W:]
    bd = bd_ref[...]

    def segsum(t):
        t1 = t.astype(BF16)
        t2 = (t - t1.astype(F32)).astype(BF16)
        return _dot(t1, bd) + _dot(t2, bd)

    qn = q * lax.rsqrt(segsum(q * q) + NORM_EPS) * (GDN_DK ** -0.5)
    kn = k * lax.rsqrt(segsum(k * k) + NORM_EPS)

    ab = ab_ref[...]
    g128 = -jnp.exp(alog_ref[...]) * _softplus(ab + dtb_ref[...])
    beta128 = jax.nn.sigmoid(ab)
    gcum = _dot_rhs_exact(lblk_ref[...], g128)
    gc_e = _dot_lhs_exact(gcum, eg_ref[...])
    beta_e = _dot_lhs_exact(beta128, eb_ref[...])
    eg_e = jnp.exp(gc_e)
    kb = kn * beta_e
    vb = v * beta_e
    kbeg = kb * eg_e
    qg = qn * eg_e

    lane = lax.broadcasted_iota(I32, (c, LANE), 1)
    sub = lax.broadcasted_iota(I32, (c, LANE), 0)
    jj = jnp.where(lane >= GDN_DK, lane - GDN_DK, lane)
    m_incl = sub >= jj
    m_strict = sub > jj
    m_diag = sub == jj
    lane2 = lax.broadcasted_iota(I32, (2 * c, LANE), 1)
    sub2 = lax.broadcasted_iota(I32, (2 * c, LANE), 0)
    m_bd = (lane2 >= GDN_DK) == (sub2 >= GDN_DK)
    eye_bd = (lane2 == sub2).astype(F32)

    def to_bd(xp):
        return jnp.where(m_bd, jnp.concatenate([xp, xp], axis=0), 0.0)

    def from_bd(xb):
        return xb[0:c, :] + xb[c:2 * c, :]

    blocks = [(ci, p) for ci in range(GDN_NC) for p in range(GDN_PAIRS)]

    def sl(ci, p):
        return slice(ci * c, (ci + 1) * c), slice(p * LANE, (p + 1) * LANE)

    gcols = [gc_e[sl(*b)] for b in blocks]
    decs = []
    for gcol in gcols:
        grow = jnp.sum(jnp.where(m_diag, gcol, 0.0), axis=0, keepdims=True)
        decs.append(jnp.where(m_incl, jnp.exp(jnp.where(m_incl, gcol - grow, 0.0)), 0.0))
    aqs = [_dot_nt(jnp.concatenate([kb[sl(*b)], qn[sl(*b)]], axis=0).astype(BF16),
                   to_bd(kn[sl(*b)]).astype(BF16)) for b in blocks]
    qks = [aq[c:2 * c, :] * dec for aq, dec in zip(aqs, decs)]
    a_bds = [to_bd(jnp.where(m_strict, aq[0:c, :] * dec, 0.0)) for aq, dec in zip(aqs, decs)]
    t_bds = [eye_bd - a for a in a_bds]
    pws = [a.astype(BF16) for a in a_bds]
    for _ in range(5):
        pws = [_dot(pw, pw).astype(BF16) for pw in pws]
        t_bds = [t + _dot(t.astype(BF16), pw) for t, pw in zip(t_bds, pws)]
    uws = [_dot(t.astype(BF16), jnp.concatenate([to_bd(vb[sl(*b)]), to_bd(kbeg[sl(*b)])], axis=1).astype(BF16))
           for t, b in zip(t_bds, blocks)]
    us = [from_bd(uw[:, 0:LANE]) for uw in uws]
    ws_ = [from_bd(uw[:, LANE:2 * LANE]) for uw in uws]

    states = [s_ref[p] for p in range(GDN_PAIRS)]
    o_parts = []
    for ci in range(GDN_NC):
        idx = [ci * GDN_PAIRS + p for p in range(GDN_PAIRS)]
        wss = [_dot(jnp.concatenate([ws_[i], qg[sl(*blocks[i])]], axis=0).astype(BF16), states[p].astype(BF16))
               for p, i in enumerate(idx)]
        vns = [us[i] - wss[p][0:c, :] for p, i in enumerate(idx)]
        o_pairs = [wss[p][c:2 * c, :] + _dot(qks[i].astype(BF16), to_bd(vns[p]).astype(BF16))
                   for p, i in enumerate(idx)]
        for p, i in enumerate(idx):
            rs, ls = sl(*blocks[i])
            glast = gc_e[ci * c + c - 1:ci * c + c, ls]
            kg = (kn[rs, ls] * jnp.exp(glast - gcols[i])).astype(BF16)
            upd = _dot_tn(kg, vns[p].astype(BF16))
            states[p] = states[p] * jnp.exp(glast) + jnp.where(m_bd, upd, 0.0)
        o_parts.append(jnp.concatenate(o_pairs, axis=1))
    o = jnp.concatenate(o_parts, axis=0)
    ms = segsum(o * o) * (1.0 / GDN_DV)
    z = z_ref[...]
    o_ref[...] = (o * lax.rsqrt(ms + NORM_EPS) * gn_ref[...] * (z * jax.nn.sigmoid(z))).astype(o_ref.dtype)
    for p in range(GDN_PAIRS):
        s_ref[p] = states[p]
        ssm_ref[p] = from_bd(states[p])


def _gdn_prompt(qkv, z, ab, conv_w, a_log, dt_bias, gnorm):
    n = qkv.shape[0]
    tb = GDN_TB
    bd, lblk, eg, eb = _gdn_tables(tb)
    pad8 = lambda t: jnp.concatenate([t.reshape(1, -1), jnp.zeros((1, LANE - t.size), F32)], axis=1)
    alog128 = pad8(a_log)
    dtb128 = pad8(dt_bias)
    gn = jnp.tile(gnorm.reshape(1, GDN_DV), (1, GDN_HEADS))
    row = lambda i: (i, 0)
    const = lambda i: (0, 0)
    o, conv, ssm = pl.pallas_call(
        _gdn_kernel,
        grid=(n // tb,),
        in_specs=[pl.BlockSpec((tb, GDN_CONV_W), row), pl.BlockSpec((tb, GDN_V_W), row),
                  pl.BlockSpec((tb, LANE), row), pl.BlockSpec((GDN_CONV, GDN_CONV_W), const),
                  pl.BlockSpec((1, LANE), const), pl.BlockSpec((1, LANE), const),
                  pl.BlockSpec((1, GDN_V_W), const), pl.BlockSpec((GDN_QK_W, GDN_QK_W), const),
                  pl.BlockSpec((tb, tb), const), pl.BlockSpec((LANE, GDN_QK_W), const),
                  pl.BlockSpec((LANE, GDN_QK_W), const)],
        out_specs=[pl.BlockSpec((tb, GDN_V_W), row),
                   pl.BlockSpec((GDN_CONV - 1, GDN_CONV_W), const),
                   pl.BlockSpec((GDN_PAIRS, GDN_DK, LANE), lambda i: (0, 0, 0))],
        out_shape=[jax.ShapeDtypeStruct((n, GDN_V_W), BF16),
                   jax.ShapeDtypeStruct((GDN_CONV - 1, GDN_CONV_W), F32),
                   jax.ShapeDtypeStruct((GDN_PAIRS, GDN_DK, LANE), F32)],
        scratch_shapes=[pltpu.VMEM((tb + TAIL, GDN_CONV_W), F32),
                        pltpu.VMEM((GDN_PAIRS, LANE, LANE), F32)],
        compiler_params=_cparams(("arbitrary",)),
        name="gdn_prompt",
    )(qkv, z, ab, conv_w, alog128, dtb128, gn, bd, lblk, eg, eb)
    ssm = ssm.reshape(GDN_PAIRS, GDN_DK, 2, GDN_DV).transpose(0, 2, 1, 3).reshape(GDN_HEADS, GDN_DK, GDN_DV)
    return o, conv, ssm


def _rope_tables(pos):
    half = MLA_ROPE // 2
    inv = 1.0 / (ROPE_THETA ** (jnp.arange(half, dtype=F32) / half))
    ang = pos.astype(F32)[:, None] * inv[None, :]
    cos, sin = jnp.cos(ang), jnp.sin(ang)
    n = pos.shape[0]
    one = jnp.ones((n, ROPE_LANE0), F32)
    zero = lambda w: jnp.zeros((n, w), F32)
    c = jnp.concatenate([one, cos, cos, jnp.ones((n, LANE - ROPE_LANE0 - MLA_ROPE), F32)], axis=1)
    sa = jnp.concatenate([zero(ROPE_LANE0), -sin, zero(LANE - ROPE_LANE0 - half)], axis=1)
    sb = jnp.concatenate([zero(ROPE_LANE0 + half), sin, zero(LANE - ROPE_LANE0 - MLA_ROPE)], axis=1)
    return c, sa, sb


def _rope_apply(x, c, sa, sb):
    w = x.shape[1]
    half = MLA_ROPE // 2
    return x * c + pltpu.roll(x, w - half, 1) * sa + pltpu.roll(x, half, 1) * sb


def _tile_heads(t, n):
    return jnp.concatenate([t] * n, axis=1)


def _pack_mla_weights(w_qb, w_kvb):
    kq = w_qb.shape[0]
    wq = w_qb.reshape(kq, MLA_HEADS, MLA_QK_HEAD)
    wq = jnp.concatenate([wq, jnp.zeros((kq, MLA_HEADS, LANE - MLA_QK_HEAD), w_qb.dtype)], axis=2)
    wq = wq.reshape(kq, MLA_HEADS * LANE)
    kk = w_kvb.shape[0]
    wkv = w_kvb.reshape(kk, MLA_HEADS, MLA_NOPE + MLA_V)
    zpad = jnp.zeros((kk, MLA_HEADS, LANE - MLA_NOPE), w_kvb.dtype)
    wk = jnp.concatenate([wkv[..., :MLA_NOPE], zpad], axis=2).reshape(kk, MLA_HEADS * LANE)
    wv = jnp.concatenate([wkv[..., MLA_NOPE:], zpad], axis=2).reshape(kk, MLA_HEADS * LANE)
    wuk = jnp.transpose(wkv[..., :MLA_NOPE], (1, 2, 0))
    wuk = jnp.concatenate([wuk, jnp.zeros((MLA_HEADS, LANE - MLA_NOPE, kk), w_kvb.dtype)], axis=1)
    wuv = jnp.transpose(wkv[..., MLA_NOPE:], (1, 0, 2))
    wuv = jnp.concatenate([wuv, jnp.zeros((MLA_HEADS, kk, LANE - MLA_V), w_kvb.dtype)], axis=2)
    return wq, wk, wv, wuk, wuv


def _mla_proj_kernel(qa_ref, lat_ref, rope_ref, c_ref, sa_ref, sb_ref, qn_ref, kvn_ref, wq_ref, wk_ref, wv_ref,
                     q_ref, k_ref, v_ref, ckv_ref, kpe_ref):
    c, sa, sb = c_ref[...], sa_ref[...], sb_ref[...]
    qa = _rms(qa_ref[...], qn_ref[...]).astype(BF16)
    q = _dot(qa, wq_ref[...])
    q = _rope_apply(q, _tile_heads(c, MLA_HEADS), _tile_heads(sa, MLA_HEADS), _tile_heads(sb, MLA_HEADS))
    q_ref[...] = (q * (MLA_SCALE * LOG2E)).astype(BF16)
    ckv = _rms(lat_ref[...], kvn_ref[...])
    ckv_ref[...] = ckv
    kpe = _rope_apply(rope_ref[...], c, sa, sb)
    kpe_ref[...] = kpe
    cb = ckv.astype(BF16)
    k_ref[...] = (_dot(cb, wk_ref[...]) + _tile_heads(kpe, MLA_HEADS)).astype(BF16)
    v_ref[...] = _dot(cb, wv_ref[...]).astype(BF16)


def _mla_proj(qa, lat, rope, tabs, q_norm, kv_norm, wq, wk, wv, tm):
    n = qa.shape[0]
    hw = MLA_HEADS * LANE
    row = lambda i: (i, 0)
    const = lambda i: (0, 0)
    return pl.pallas_call(
        _mla_proj_kernel,
        grid=(n // tm,),
        in_specs=[pl.BlockSpec((tm, MLA_Q_LORA), row), pl.BlockSpec((tm, MLA_KV_LORA), row),
                  pl.BlockSpec((tm, LANE), row), pl.BlockSpec((tm, LANE), row), pl.BlockSpec((tm, LANE), row),
                  pl.BlockSpec((tm, LANE), row), pl.BlockSpec((1, MLA_Q_LORA), const),
                  pl.BlockSpec((1, MLA_KV_LORA), const), pl.BlockSpec((MLA_Q_LORA, hw), const),
                  pl.BlockSpec((MLA_KV_LORA, hw), const), pl.BlockSpec((MLA_KV_LORA, hw), const)],
        out_specs=[pl.BlockSpec((tm, hw), row), pl.BlockSpec((tm, hw), row), pl.BlockSpec((tm, hw), row),
                   pl.BlockSpec((tm, MLA_KV_LORA), row), pl.BlockSpec((tm, LANE), row)],
        out_shape=[jax.ShapeDtypeStruct((n, hw), BF16), jax.ShapeDtypeStruct((n, hw), BF16),
                   jax.ShapeDtypeStruct((n, hw), BF16), jax.ShapeDtypeStruct((n, MLA_KV_LORA), F32),
                   jax.ShapeDtypeStruct((n, LANE), F32)],
        compiler_params=_cparams(("arbitrary",)),
        name="mla_proj",
    )(qa, lat, rope, *tabs, q_norm, kv_norm, wq, wk, wv)


ATT_TQ = 1024
ATT_TK = 512
ATT_RC = 256
NEG_BIG = -1e30


def _attn_kernel(qi_ref, kj_ref, q_ref, k_ref, v_ref, o_ref, acc_ref, m_ref, l_ref):
    step = pl.program_id(0)
    i = qi_ref[step]
    j = kj_ref[step]
    tq, tk = ATT_TQ, ATT_TK
    delta = i * tq - j * tk
    last_j = (i * tq + tq - 1) // tk

    @pl.when(j == 0)
    def _():
        acc_ref[...] = jnp.zeros_like(acc_ref)
        m_ref[...] = jnp.full_like(m_ref, NEG_BIG)
        l_ref[...] = jnp.zeros_like(l_ref)

    rc = ATT_RC
    nlc = tk // LANE

    def update(masked):
        if masked:
            diff = lax.broadcasted_iota(I32, (rc, LANE), 1) - lax.broadcasted_iota(I32, (rc, LANE), 0)
        rss = [slice(r * rc, (r + 1) * rc) for r in range(tq // rc)]

        def scores(h):
            hs = slice(h * LANE, (h + 1) * LANE)
            kh = k_ref[:, hs]
            return [_dot_nt(q_ref[rs, hs], kh) for rs in rss]

        ss_next = scores(0)
        for h in range(MLA_HEADS):
            hs = slice(h * LANE, (h + 1) * LANE)
            ss = ss_next
            if h + 1 < MLA_HEADS:
                ss_next = scores(h + 1)
            vh = v_ref[:, hs]
            scs = [[s[:, c * LANE:(c + 1) * LANE] for c in range(nlc)] for s in ss]
            if masked:
                scs = [[jnp.where(diff <= delta + (r * rc - c * LANE), sc[c], NEG_BIG) for c in range(nlc)]
                       for r, sc in enumerate(scs)]
            m_prevs = [m_ref[h, rs] for rs in rss]
            m_news = []
            for sc, m_prev in zip(scs, m_prevs):
                m_lane = sc[0]
                for c in range(1, nlc):
                    m_lane = jnp.maximum(m_lane, sc[c])
                m_news.append(jnp.maximum(m_prev, jnp.max(m_lane, axis=1, keepdims=True)))
            alphas = [jnp.exp2(m_prev - m_new) for m_prev, m_new in zip(m_prevs, m_news)]
            pcs = [[jnp.exp2(t - m_new) for t in sc] for sc, m_new in zip(scs, m_news)]
            for rs, pc, alpha, m_new in zip(rss, pcs, alphas, m_news):
                p_lane = pc[0]
                for c in range(1, nlc):
                    p_lane = p_lane + pc[c]
                l_ref[h, rs] = alpha * l_ref[h, rs] + p_lane
                m_ref[h, rs] = m_new
            pvs = [_dot(jnp.concatenate([t.astype(BF16) for t in pc], axis=1), vh) for pc in pcs]
            for rs, alpha, pv in zip(rss, alphas, pvs):
                acc_ref[rs, hs] = alpha * acc_ref[rs, hs] + pv

    fully_visible = (j * tk + tk - 1) <= i * tq

    @pl.when(fully_visible)
    def _():
        update(False)

    @pl.when(jnp.logical_not(fully_visible))
    def _():
        update(True)

    @pl.when(j == last_j)
    def _():
        for h in range(MLA_HEADS):
            hs = slice(h * LANE, (h + 1) * LANE)
            l_row = jnp.sum(l_ref[h], axis=1, keepdims=True)
            o_ref[:, hs] = (acc_ref[:, hs] / l_row).astype(o_ref.dtype)


def _attn_prompt(q, k, v):
    n = q.shape[0]
    tq, tk = ATT_TQ, ATT_TK
    nq = n // tq
    nk_of = [(i * tq + tq - 1) // tk + 1 for i in range(nq)]
    qi = np.concatenate([np.full(nk_of[i], i) for i in range(nq)]).astype(np.int32)
    kj = np.concatenate([np.arange(nk_of[i]) for i in range(nq)]).astype(np.int32)
    hw = MLA_HEADS * LANE
    grid_spec = pltpu.PrefetchScalarGridSpec(
        num_scalar_prefetch=2,
        grid=(len(qi),),
        in_specs=[pl.BlockSpec((tq, hw), lambda s, qi, kj: (qi[s], 0)),
                  pl.BlockSpec((tk, hw), lambda s, qi, kj: (kj[s], 0)),
                  pl.BlockSpec((tk, hw), lambda s, qi, kj: (kj[s], 0))],
        out_specs=pl.BlockSpec((tq, hw), lambda s, qi, kj: (qi[s], 0)),
        scratch_shapes=[pltpu.VMEM((tq, hw), F32), pltpu.VMEM((MLA_HEADS, tq, LANE), F32),
                        pltpu.VMEM((MLA_HEADS, tq, LANE), F32)],
    )
    return pl.pallas_call(
        _attn_kernel,
        grid_spec=grid_spec,
        out_shape=jax.ShapeDtypeStruct((n, hw), BF16),
        compiler_params=_cparams(("arbitrary",)),
        name="mla_attn_prompt",
    )(jnp.asarray(qi), jnp.asarray(kj), q, k, v)


def _memkv_kernel(mem_ref, g_ref, wk_ref, wv_ref, k_ref, v_ref):
    m = _rms(mem_ref[...], g_ref[...]).astype(BF16)
    k_ref[...] = _dot(m, wk_ref[...])
    v_ref[...] = _dot(m, wv_ref[...])


def _memkv(mem, g, wk, wv):
    m = mem.shape[0]
    return pl.pallas_call(
        _memkv_kernel,
        out_shape=[jax.ShapeDtypeStruct((m, MEM_W), F32), jax.ShapeDtypeStruct((m, MEM_W), F32)],
        compiler_params=pltpu.CompilerParams(vmem_limit_bytes=VMEM_LIMIT),
        name="mem_kv",
    )(mem, g, wk, wv)


POST_TM = 512


def _mem_attend_shared(qb, mk, mv):
    outs = []
    for h in range(MEM_HEADS):
        hs = slice(h * MEM_HEAD_DIM, (h + 1) * MEM_HEAD_DIM)
        s = _dot_nt(qb[:, hs], mk[:, hs])
        p = jnp.exp2(s - jnp.max(s, axis=1, keepdims=True))
        l = jnp.sum(p, axis=1, keepdims=True)
        outs.append(_dot(p.astype(BF16), mv[:, hs]) / l)
    return jnp.concatenate(outs, axis=1)


def _router(h3, rw, rb, base, lstrict):
    tm = h3.shape[0]
    logits = jnp.dot(h3, rw, precision=lax.Precision.HIGHEST, preferred_element_type=F32) + rb
    lane = lax.broadcasted_iota(I32, (tm, LANE), 1)
    lane_f = lane.astype(F32)
    work = jnp.where(lane < N_EXPERTS, logits, -jnp.inf)
    vals, ohs, idxs = [], [], []
    for _ in range(TOP_K):
        mx = jnp.max(work, axis=1, keepdims=True)
        idx = jnp.min(jnp.where(work == mx, lane_f, float(LANE)), axis=1, keepdims=True)
        oh = lane_f == idx
        work = jnp.where(oh, -jnp.inf, work)
        vals.append(mx)
        idxs.append(idx)
        ohs.append(oh)
    es = [jnp.exp(v - vals[0]) for v in vals]
    denom = es[0] + es[1] + es[2] + es[3]
    sel = jnp.zeros((tm, LANE), F32)
    for oh in ohs:
        sel = sel + oh.astype(F32)
    rank_te = _dot(lstrict, sel.astype(BF16)) + base
    idx_out = jnp.zeros((tm, LANE), F32)
    w_out = jnp.zeros((tm, LANE), F32)
    r_out = jnp.zeros((tm, LANE), F32)
    for k in range(TOP_K):
        rk = jnp.sum(jnp.where(ohs[k], rank_te, 0.0), axis=1, keepdims=True)
        idx_out = jnp.where(lane == k, idxs[k], idx_out)
        w_out = jnp.where(lane == k, es[k] / denom, w_out)
        r_out = jnp.where(lane == k, rk, r_out)
    new_base = base + jnp.sum(sel, axis=0, keepdims=True)
    return idx_out.astype(I32), w_out, r_out.astype(I32), new_base


def _post_prompt_kernel(x_ref, og_ref, om_ref, sga_ref, sgb_ref, wgu_ref, wmu_ref, wout_ref, nmem_ref, wmq_ref,
                        mk_ref, mv_ref, wmo_ref, nffn_ref, rw_ref, rb_ref, base0_ref, ls_ref,
                        x2_ref, h3_ref, idx_ref, wts_ref, rank_ref, cnt_ref, base_ref):
    @pl.when(pl.program_id(0) == 0)
    def _():
        base_ref[...] = base0_ref[...]

    tm = x_ref.shape[0]
    halves = [slice(0, tm // 2), slice(tm // 2, tm)]
    gas = [sga_ref[hs, :].astype(F32) * _dot(og_ref[hs, :], wgu_ref[...]) for hs in halves]
    gbs = [sgb_ref[hs, :].astype(F32) * _dot(om_ref[hs, :], wmu_ref[...]) for hs in halves]
    x1s = [x_ref[hs, :] + _dot((ga + gb).astype(BF16), wout_ref[...]) for hs, ga, gb in zip(halves, gas, gbs)]
    qbs = [(_dot(_rms(x1, nmem_ref[...]).astype(BF16), wmq_ref[...]) * (MEM_HEAD_DIM ** -0.5 * LOG2E)).astype(BF16)
           for x1 in x1s]
    oms = [_mem_attend_shared(qb, mk_ref[...], mv_ref[...]) for qb in qbs]
    x2s = [x1 + _dot(om.astype(BF16), wmo_ref[...]) for x1, om in zip(x1s, oms)]
    h3s = [_rms(x2, nffn_ref[...]) for x2 in x2s]
    for hs, x2, h3h in zip(halves, x2s, h3s):
        x2_ref[hs, :] = x2
        h3_ref[hs, :] = h3h
    h3 = jnp.concatenate(h3s, axis=0)
    idx, wts, rank, nb = _router(h3, rw_ref[...], rb_ref[...], base_ref[...], ls_ref[...])
    idx_ref[...] = idx
    wts_ref[...] = wts
    rank_ref[...] = rank
    base_ref[...] = nb
    cnt_ref[...] = nb


def _strict_lower(tm):
    r = np.arange(tm)
    return jnp.asarray((r[:, None] > r[None, :]).astype(np.float32), BF16)


def _post_prompt(x, og, om, sga, sgb, wgu, wmu, wout, nmem, wmq, mk, mv, wmo, nffn, rw, rb, base0):
    n = x.shape[0]
    tm = POST_TM
    row = lambda i: (i, 0)
    const = lambda i: (0, 0)
    full = lambda a: pl.BlockSpec(a.shape, const)
    ls = _strict_lower(tm)
    ins = [x, og, om, sga, sgb, wgu, wmu, wout, nmem, wmq, mk, mv, wmo, nffn, rw, rb, base0, ls]
    in_specs = [pl.BlockSpec((tm, a.shape[1]), row) for a in ins[:5]] + [full(a) for a in ins[5:]]
    return pl.pallas_call(
        _post_prompt_kernel,
        grid=(n // tm,),
        in_specs=in_specs,
        out_specs=[pl.BlockSpec((tm, D_MODEL), row), pl.BlockSpec((tm, D_MODEL), row),
                   pl.BlockSpec((tm, LANE), row), pl.BlockSpec((tm, LANE), row), pl.BlockSpec((tm, LANE), row),
                   pl.BlockSpec((1, LANE), const)],
        out_shape=[jax.ShapeDtypeStruct((n, D_MODEL), F32), jax.ShapeDtypeStruct((n, D_MODEL), F32),
                   jax.ShapeDtypeStruct((n, LANE), I32), jax.ShapeDtypeStruct((n, LANE), F32),
                   jax.ShapeDtypeStruct((n, LANE), I32), jax.ShapeDtypeStruct((1, LANE), F32)],
        scratch_shapes=[pltpu.VMEM((1, LANE), F32)],
        compiler_params=_cparams(("arbitrary",)),
        name="post_prompt",
    )(*ins)


MOE_TE = 512
MOE_TM = 256


def _route_plan(counts, n_rows):
    te = MOE_TE
    padded = ((counts + te - 1) // te) * te
    ends = jnp.cumsum(padded)
    offs = ends - padded
    n_tiles = n_rows // te + N_EXPERTS
    tile_start = jnp.arange(n_tiles, dtype=I32) * te
    tile_e = jnp.minimum(jnp.sum((tile_start[:, None] >= ends[None, :]).astype(I32), axis=1), N_EXPERTS - 1)
    tile_nv = jnp.clip(offs[tile_e] + counts[tile_e] - tile_start, 0, te).astype(I32)
    return offs.astype(I32), tile_e, tile_nv, n_tiles


def _dispatch_kernel(pos_ref, h_ref, xs_in_ref, xs_ref, sem):
    del xs_in_ref
    tm = h_ref.shape[0]

    def body(r, carry):
        for k in range(TOP_K):
            p = pos_ref[TOP_K * r + k]
            pltpu.make_async_copy(h_ref.at[pl.ds(r, 1)], xs_ref.at[pl.ds(p, 1)], sem).start()
        return carry

    lax.fori_loop(0, tm, body, 0)
    for _ in range(TOP_K):
        pltpu.make_async_copy(h_ref, xs_ref.at[pl.ds(0, tm)], sem).wait()


def _dispatch(pos, h3, xs, tm):
    n = h3.shape[0]
    return pl.pallas_call(
        _dispatch_kernel,
        grid=(n // tm,),
        in_specs=[pl.BlockSpec((tm * TOP_K,), lambda i: (i,), memory_space=pltpu.SMEM),
                  pl.BlockSpec((tm, D_MODEL), lambda i: (i, 0)),
                  pl.BlockSpec(memory_space=pl.ANY)],
        out_specs=pl.BlockSpec(memory_space=pl.ANY),
        out_shape=jax.ShapeDtypeStruct(xs.shape, xs.dtype),
        scratch_shapes=[pltpu.SemaphoreType.DMA],
        input_output_aliases={2: 0},
        compiler_params=_cparams(("arbitrary",)),
        name="moe_dispatch",
    )(pos, h3, xs)


def _ffn_kernel(te_ref, nv_ref, x_ref, wg_ref, bg_ref, wu_ref, bu_ref, wd_ref, bd_ref, y_ref,
                wgb_ref, wub_ref, wdb_ref):
    t = pl.program_id(0)
    e = te_ref[t]
    nv = nv_ref[t]
    changed = jnp.logical_or(t == 0, te_ref[jnp.maximum(t - 1, 0)] != e)

    @pl.when(changed)
    def _():
        wgb_ref[...] = wg_ref[...].astype(BF16)
        wub_ref[...] = wu_ref[...].astype(BF16)
        wdb_ref[...] = wd_ref[...].astype(BF16)

    half = MOE_TE // 2
    lo, hi = slice(0, half), slice(half, MOE_TE)

    def ffn_rows(sls):
        rows = lax.broadcasted_iota(I32, (half, D_MODEL), 0)
        xbs = [jnp.where(rows + sl.start < nv, x_ref[sl, :], 0.0).astype(BF16) for sl in sls]
        gates = [jnp.minimum(_dot(xb, wgb_ref[...]) + bg_ref[...], SWIGLU_LIMIT) for xb in xbs]
        ups = [jnp.clip(_dot(xb, wub_ref[...]) + bu_ref[...], -SWIGLU_LIMIT, SWIGLU_LIMIT) for xb in xbs]
        acts = [((up + 1.0) * gate * jax.nn.sigmoid(SWIGLU_ALPHA * gate)).astype(BF16)
                for up, gate in zip(ups, gates)]
        for sl, act in zip(sls, acts):
            y_ref[sl, :] = _dot(act, wdb_ref[...]) + bd_ref[...]

    @pl.when(nv > half)
    def _():
        ffn_rows([lo, hi])

    @pl.when(jnp.logical_and(nv > 0, nv <= half))
    def _():
        ffn_rows([lo])
        y_ref[hi, :] = jnp.zeros((half, D_MODEL), F32)

    @pl.when(nv == 0)
    def _():
        y_ref[...] = jnp.zeros_like(y_ref)


def _ffn(tile_e, tile_nv, xs, w_gate, b_gate, w_up, b_up, w_down, b_down, n_tiles):
    te = MOE_TE
    wspec = lambda: pl.BlockSpec((None, D_MODEL, D_FF), lambda t, te_, nv_: (te_[t], 0, 0))
    bspec = lambda: pl.BlockSpec((None, 1, D_FF), lambda t, te_, nv_: (te_[t], 0, 0))
    grid_spec = pltpu.PrefetchScalarGridSpec(
        num_scalar_prefetch=2,
        grid=(n_tiles,),
        in_specs=[pl.BlockSpec((te, D_MODEL), lambda t, te_, nv_: (t, 0)),
                  wspec(), bspec(), wspec(), bspec(), wspec(), bspec()],
        out_specs=pl.BlockSpec((te, D_MODEL), lambda t, te_, nv_: (t, 0)),
        scratch_shapes=[pltpu.VMEM((D_MODEL, D_FF), BF16), pltpu.VMEM((D_MODEL, D_FF), BF16),
                        pltpu.VMEM((D_FF, D_MODEL), BF16)],
    )
    return pl.pallas_call(
        _ffn_kernel,
        grid_spec=grid_spec,
        out_shape=jax.ShapeDtypeStruct(xs.shape, F32),
        compiler_params=_cparams(("arbitrary",)),
        name="moe_ffn",
    )(tile_e, tile_nv, xs, w_gate, b_gate.reshape(N_EXPERTS, 1, D_FF), w_up, b_up.reshape(N_EXPERTS, 1, D_FF),
      w_down, b_down.reshape(N_EXPERTS, 1, D_MODEL))


def _combine_kernel(pos_ref, posn_ref, x2_ref, w_ref, nf_ref, ys_ref, y_ref, buf_ref, sem):
    tm = x2_ref.shape[0]
    i = pl.program_id(0)
    slot = lax.rem(i, 2)

    def gather(p_ref, s):
        def body(r, carry):
            for k in range(TOP_K):
                p = p_ref[TOP_K * r + k]
                pltpu.make_async_copy(ys_ref.at[pl.ds(p, 1)], buf_ref.at[s, k, pl.ds(r, 1)], sem.at[s]).start()
            return carry

        lax.fori_loop(0, tm, body, 0)

    @pl.when(i == 0)
    def _():
        gather(pos_ref, 0)

    @pl.when(i + 1 < pl.num_programs(0))
    def _():
        gather(posn_ref, 1 - slot)

    for k in range(TOP_K):
        pltpu.make_async_copy(ys_ref.at[pl.ds(0, tm)], buf_ref.at[slot, k], sem.at[slot]).wait()
    w = w_ref[...]
    y = x2_ref[...]
    for k in range(TOP_K):
        y = y + w[:, k:k + 1] * buf_ref[slot, k]
    y_ref[...] = _rms(y, nf_ref[...])


def _combine(pos, x2, wts, nf, ys, tm):
    n = x2.shape[0]
    last = n // tm - 1
    return pl.pallas_call(
        _combine_kernel,
        grid=(n // tm,),
        in_specs=[pl.BlockSpec((tm * TOP_K,), lambda i: (i,), memory_space=pltpu.SMEM),
                  pl.BlockSpec((tm * TOP_K,), lambda i: (jnp.minimum(i + 1, last),), memory_space=pltpu.SMEM),
                  pl.BlockSpec((tm, D_MODEL), lambda i: (i, 0)),
                  pl.BlockSpec((tm, LANE), lambda i: (i, 0)),
                  pl.BlockSpec((1, D_MODEL), lambda i: (0, 0)),
                  pl.BlockSpec(memory_space=pl.ANY)],
        out_specs=pl.BlockSpec((tm, D_MODEL), lambda i: (i, 0)),
        out_shape=jax.ShapeDtypeStruct((n, D_MODEL), F32),
        scratch_shapes=[pltpu.VMEM((2, TOP_K, tm, D_MODEL), F32), pltpu.SemaphoreType.DMA((2,))],
        compiler_params=_cparams(("arbitrary",)),
        name="moe_combine",
    )(pos, pos, x2, wts, nf, ys)


def _gdn_dec_prep_kernel(qkv_ref, cs_ref, z_ref, ab_ref, cw_ref, alog_ref, dtb_ref, bd_ref, eg_ref, eb_ref,
                         csn_ref, q_ref, k_ref, v_ref, ege_ref, be_ref, zs_ref):
    x = qkv_ref[...]
    y = cw_ref[GDN_CONV - 1:GDN_CONV, :] * x
    for j in range(GDN_CONV - 1):
        y = y + cw_ref[j:j + 1, :] * cs_ref[:, j, :]
    for j in range(GDN_CONV - 2):
        csn_ref[:, j, :] = cs_ref[:, j + 1, :]
    csn_ref[:, GDN_CONV - 2, :] = x
    act = y * jax.nn.sigmoid(y)
    q = act[:, 0:GDN_QK_W]
    k = act[:, GDN_QK_W:2 * GDN_QK_W]
    bd = bd_ref[...]
    q_ref[...] = q * lax.rsqrt(_dot_lhs_exact(q * q, bd) + NORM_EPS) * (GDN_DK ** -0.5)
    k_ref[...] = k * lax.rsqrt(_dot_lhs_exact(k * k, bd) + NORM_EPS)
    v_ref[...] = act[:, 2 * GDN_QK_W:]
    ab = ab_ref[...]
    g128 = -jnp.exp(alog_ref[...]) * _softplus(ab + dtb_ref[...])
    ege_ref[...] = jnp.exp(_dot_lhs_exact(g128, eg_ref[...]))
    be_ref[...] = _dot_lhs_exact(jax.nn.sigmoid(ab), eb_ref[...])
    z = z_ref[...]
    zs_ref[...] = z * jax.nn.sigmoid(z)


def _gdn_dec_prep(qkv, cs, z, ab, conv_w, alog128, dtb128):
    b = qkv.shape[0]
    bd, _, eg, eb = _gdn_tables(GDN_CHUNK)
    wide = jax.ShapeDtypeStruct((b, GDN_QK_W), F32)
    return pl.pallas_call(
        _gdn_dec_prep_kernel,
        out_shape=[jax.ShapeDtypeStruct(cs.shape, F32)] + [wide] * 6,
        compiler_params=pltpu.CompilerParams(vmem_limit_bytes=VMEM_LIMIT),
        name="gdn_decode_prep",
    )(qkv, cs, z, ab, conv_w, alog128, dtb128, bd, eg, eb)


GDN_DEC_B = 64


def _gdn_dec_step_kernel(s_ref, q_ref, k_ref, v_ref, eg_ref, be_ref, zs_ref, gn_ref, sn_ref, o_ref):
    s = s_ref[...]
    q, k, v = q_ref[...], k_ref[...], v_ref[...]
    eg, be = eg_ref[...], be_ref[...]
    eye = lax.broadcasted_iota(I32, (GDN_DK, GDN_DK), 0) == lax.broadcasted_iota(I32, (GDN_DK, GDN_DK), 1)
    kcol = jnp.sum(jnp.where(eye, k, 0.0), axis=2, keepdims=True)
    qcol = jnp.sum(jnp.where(eye, q, 0.0), axis=2, keepdims=True)
    ks = jnp.sum(s * kcol, axis=1, keepdims=True)
    qs = jnp.sum(s * qcol, axis=1, keepdims=True)
    vn = be * (v - eg * ks)
    qk = jnp.sum(q * k, axis=2, keepdims=True)
    o = eg * qs + qk * vn
    sn_ref[...] = s * eg + kcol * vn
    ms = jnp.mean(o * o, axis=2, keepdims=True)
    o_ref[...] = o * lax.rsqrt(ms + NORM_EPS) * gn_ref[...] * zs_ref[...]


def _gdn_dec_step(s0, rows, gn):
    bh = s0.shape[0]
    bb = GDN_DEC_B
    sspec = pl.BlockSpec((bb, GDN_DK, GDN_DV), lambda i: (i, 0, 0))
    rspec = pl.BlockSpec((bb, 1, GDN_DV), lambda i: (i, 0, 0))
    return pl.pallas_call(
        _gdn_dec_step_kernel,
        grid=(bh // bb,),
        in_specs=[sspec] + [rspec] * 6 + [pl.BlockSpec((1, 1, GDN_DV), lambda i: (0, 0, 0))],
        out_specs=[sspec, rspec],
        out_shape=[jax.ShapeDtypeStruct(s0.shape, F32), jax.ShapeDtypeStruct((bh, 1, GDN_DV), F32)],
        compiler_params=_cparams(("arbitrary",)),
        name="gdn_decode_step",
    )(s0, *rows, gn)


def _mla_dec_proj_kernel(qa_ref, lat_ref, rope_ref, c_ref, sa_ref, sb_ref, qn_ref, kvn_ref, wq_ref, wuk_ref,
                         qlat_ref, q_ref, ckv_ref, kpe_ref):
    c, sa, sb = c_ref[...], sa_ref[...], sb_ref[...]
    q = _dot_f32(_rms(qa_ref[...], qn_ref[...]), wq_ref[...])
    q = _rope_apply(q, _tile_heads(c, MLA_HEADS), _tile_heads(sa, MLA_HEADS), _tile_heads(sb, MLA_HEADS))
    q = q * (MLA_SCALE * LOG2E)
    q_ref[...] = q
    for h in range(MLA_HEADS):
        qlat_ref[:, h * MLA_KV_LORA:(h + 1) * MLA_KV_LORA] = _dot_f32(q[:, h * LANE:(h + 1) * LANE], wuk_ref[h])
    ckv_ref[...] = _rms(lat_ref[...], kvn_ref[...])
    kpe_ref[...] = _rope_apply(rope_ref[...], c, sa, sb)


def _mla_dec_proj(qa, lat, rope, tabs, q_norm, kv_norm, wq, wuk):
    b = qa.shape[0]
    return pl.pallas_call(
        _mla_dec_proj_kernel,
        out_shape=[jax.ShapeDtypeStruct((b, MLA_HEADS * MLA_KV_LORA), F32),
                   jax.ShapeDtypeStruct((b, MLA_HEADS * LANE), F32),
                   jax.ShapeDtypeStruct((b, MLA_KV_LORA), F32), jax.ShapeDtypeStruct((b, LANE), F32)],
        compiler_params=pltpu.CompilerParams(vmem_limit_bytes=VMEM_LIMIT),
        name="mla_decode_proj",
    )(qa, lat, rope, *tabs, q_norm, kv_norm, wq, wuk)


PAGE_GROUP = 32


def _paged_attn_kernel(pt_ref, qlat_ref, qpe_ref, ckv_ref, kpe_ref, lat_hbm, rope_hbm, o_ref,
                       latbuf, ropebuf, sem):
    b = pl.program_id(0)
    nb = pl.num_programs(0)
    n_pages = pt_ref.shape[0] // nb
    ngroups = n_pages // PAGE_GROUP
    total = nb * ngroups

    def copies(gidx, slot):
        out = []
        for g in range(PAGE_GROUP):
            page = pt_ref[gidx * PAGE_GROUP + g]
            out.append(pltpu.make_async_copy(lat_hbm.at[page], latbuf.at[slot, g], sem.at[slot]))
            out.append(pltpu.make_async_copy(rope_hbm.at[page], ropebuf.at[slot, g], sem.at[slot]))
        return out

    @pl.when(b == 0)
    def _():
        for cp in copies(0, 0):
            cp.start()

    q = qlat_ref[0]
    qp = qpe_ref[0]
    nh = MLA_HEADS

    def hi_lo(x):
        hi = x.astype(BF16)
        return jnp.concatenate([hi, (x - hi.astype(F32)).astype(BF16)], axis=0)

    q2, qp2 = hi_lo(q), hi_lo(qp)

    def body(t, carry):
        m, l, acc = carry
        gidx = b * ngroups + t
        slot = lax.rem(gidx, 2)

        @pl.when(gidx + 1 < total)
        def _():
            for cp in copies(gidx + 1, 1 - slot):
                cp.start()

        for cp in copies(gidx, slot):
            cp.wait()
        lats = [latbuf[slot, g].astype(BF16) for g in range(PAGE_GROUP)]
        ropes = [ropebuf[slot, g].astype(BF16) for g in range(PAGE_GROUP)]
        ss = [_dot_nt(q2, latg) for latg in lats]
        sr = [_dot(qp2, ropeg) for ropeg in ropes]
        s2 = jnp.concatenate([a + b_ for a, b_ in zip(ss, sr)], axis=1)
        s = s2[0:nh] + s2[nh:2 * nh]
        m_new = jnp.maximum(m, jnp.max(s, axis=1, keepdims=True))
        alpha = jnp.exp2(m - m_new)
        p = jnp.exp2(s - m_new)
        l = alpha * l + jnp.sum(p, axis=1, keepdims=True)
        pb = hi_lo(p)
        pvs = [_dot(pb[:, g * PAGE_SIZE:(g + 1) * PAGE_SIZE], lats[g]) for g in range(PAGE_GROUP)]
        while len(pvs) > 1:
            pvs = [pvs[t] + pvs[t + 1] for t in range(0, len(pvs), 2)]
        pv = pvs[0]
        return m_new, l, alpha * acc + pv[0:nh] + pv[nh:2 * nh]

    init = (jnp.full((MLA_HEADS, 1), NEG_BIG, F32), jnp.zeros((MLA_HEADS, 1), F32),
            jnp.zeros((MLA_HEADS, MLA_KV_LORA), F32))
    m, l, acc = lax.fori_loop(0, ngroups, body, init)
    ck = ckv_ref[0]
    kp = kpe_ref[0]
    s_new = jnp.sum(q * ck, axis=1, keepdims=True) + jnp.sum(qp * kp, axis=1, keepdims=True)
    m_new = jnp.maximum(m, s_new)
    alpha = jnp.exp2(m - m_new)
    p_new = jnp.exp2(s_new - m_new)
    l = alpha * l + p_new
    o_ref[0] = (alpha * acc + p_new * ck) / l


def _paged_attn(page_table, qlat, qpe, ckv, kpe, pool_lat, pool_rope):
    b, n_pages = page_table.shape
    grid_spec = pltpu.PrefetchScalarGridSpec(
        num_scalar_prefetch=1,
        grid=(b,),
        in_specs=[pl.BlockSpec((1, MLA_HEADS, MLA_KV_LORA), lambda i, pt: (i, 0, 0)),
                  pl.BlockSpec((1, MLA_HEADS, MLA_ROPE), lambda i, pt: (i, 0, 0)),
                  pl.BlockSpec((1, 1, MLA_KV_LORA), lambda i, pt: (i, 0, 0)),
                  pl.BlockSpec((1, 1, MLA_ROPE), lambda i, pt: (i, 0, 0)),
                  pl.BlockSpec(memory_space=pl.ANY), pl.BlockSpec(memory_space=pl.ANY)],
        out_specs=pl.BlockSpec((1, MLA_HEADS, MLA_KV_LORA), lambda i, pt: (i, 0, 0)),
        scratch_shapes=[pltpu.VMEM((2, PAGE_GROUP, PAGE_SIZE, MLA_KV_LORA), F32),
                        pltpu.VMEM((2, PAGE_GROUP, MLA_ROPE, PAGE_SIZE), F32),
                        pltpu.SemaphoreType.DMA((2,))],
    )
    return pl.pallas_call(
        _paged_attn_kernel,
        grid_spec=grid_spec,
        out_shape=jax.ShapeDtypeStruct((b, MLA_HEADS, MLA_KV_LORA), F32),
        compiler_params=_cparams(("arbitrary",)),
        name="mla_paged_attn",
    )(page_table.reshape(-1), qlat, qpe, ckv, kpe, pool_lat, pool_rope)


MEM_DEC_B = 4


def _mem_dec_kernel(q_ref, mk_ref, mv_ref, o_ref):
    q = q_ref[...]
    outs = []
    for h in range(MEM_HEADS):
        hs = slice(h * MEM_HEAD_DIM, (h + 1) * MEM_HEAD_DIM)
        s = jnp.sum(mk_ref[:, :, h, :] * q[:, :, hs], axis=2, keepdims=True)
        p = jnp.exp(s - jnp.max(s, axis=1, keepdims=True))
        l = jnp.sum(p, axis=1, keepdims=True)
        outs.append(jnp.sum(p * mv_ref[:, :, h, :], axis=1, keepdims=True) / l)
    o_ref[...] = jnp.concatenate(outs, axis=2)


def _mem_dec(q, mk, mv):
    b = q.shape[0]
    bb = MEM_DEC_B
    qspec = pl.BlockSpec((bb, 1, MEM_W), lambda i: (i, 0, 0))
    kspec = pl.BlockSpec((bb, MEM_TOKENS, MEM_HEADS, MEM_HEAD_DIM), lambda i: (i, 0, 0, 0))
    return pl.pallas_call(
        _mem_dec_kernel,
        grid=(b // bb,),
        in_specs=[qspec, kspec, kspec],
        out_specs=qspec,
        out_shape=jax.ShapeDtypeStruct((b, 1, MEM_W), F32),
        compiler_params=_cparams(("arbitrary",)),
        name="mem_attn_decode",
    )(q, mk, mv)


def _post_a_sample_kernel(x_ref, og_ref, olat_ref, sga_ref, sgb_ref, wuv_ref, wgu_ref, wmu_ref, wout_ref,
                          nmem_ref, wmq_ref, x1_ref, q_ref):
    olat = olat_ref[...]
    om = jnp.concatenate([_dot_f32(olat[:, h * MLA_KV_LORA:(h + 1) * MLA_KV_LORA], wuv_ref[h])
                          for h in range(MLA_HEADS)], axis=1)
    ga = sga_ref[...] * _dot_f32(og_ref[...], wgu_ref[...])
    gb = sgb_ref[...] * _dot_f32(om, wmu_ref[...])
    x1 = x_ref[...] + _dot_f32(ga + gb, wout_ref[...])
    x1_ref[...] = x1
    q_ref[...] = _dot_f32(_rms(x1, nmem_ref[...]), wmq_ref[...]) * (MEM_HEAD_DIM ** -0.5)


def _post_a_sample(x, og, olat, sga, sgb, wuv, wgu, wmu, wout, nmem, wmq):
    b = x.shape[0]
    return pl.pallas_call(
        _post_a_sample_kernel,
        out_shape=[jax.ShapeDtypeStruct((b, D_MODEL), F32), jax.ShapeDtypeStruct((b, MEM_W), F32)],
        compiler_params=pltpu.CompilerParams(vmem_limit_bytes=VMEM_LIMIT),
        name="post_a_sample",
    )(x, og, olat, sga, sgb, wuv, wgu, wmu, wout, nmem, wmq)


def _post_b_sample_kernel(x1_ref, om_ref, wmo_ref, nffn_ref, rw_ref, rb_ref, base0_ref, ls_ref,
                          x2_ref, h3_ref, idx_ref, wts_ref, rank_ref, cnt_ref):
    x2 = x1_ref[...] + _dot_f32(om_ref[...], wmo_ref[...])
    x2_ref[...] = x2
    h3 = _rms(x2, nffn_ref[...])
    h3_ref[...] = h3
    idx, wts, rank, nb = _router(h3, rw_ref[...], rb_ref[...], base0_ref[...], ls_ref[...])
    idx_ref[...] = idx
    wts_ref[...] = wts
    rank_ref[...] = rank
    cnt_ref[...] = nb


def _post_b_sample(x1, om, wmo, nffn, rw, rb, base0):
    b = x1.shape[0]
    return pl.pallas_call(
        _post_b_sample_kernel,
        out_shape=[jax.ShapeDtypeStruct((b, D_MODEL), F32), jax.ShapeDtypeStruct((b, D_MODEL), F32),
                   jax.ShapeDtypeStruct((b, LANE), I32), jax.ShapeDtypeStruct((b, LANE), F32),
                   jax.ShapeDtypeStruct((b, LANE), I32), jax.ShapeDtypeStruct((1, LANE), F32)],
        compiler_params=pltpu.CompilerParams(vmem_limit_bytes=VMEM_LIMIT),
        name="post_b_sample",
    )(x1, om, wmo, nffn, rw, rb, base0, _strict_lower(b))


def _pad_lanes(t, width=LANE):
    t = t.reshape(1, -1)
    return jnp.concatenate([t, jnp.zeros((1, width - t.shape[1]), t.dtype)], axis=1)


def kernel(x_prompt, x_sample, mem_prompt, cache_kv_latent, cache_k_rope, cache_mem_k, cache_mem_v,
           state_gdn_conv, state_gdn_ssm, page_table,
           norm_mix, w_in, gdn_conv_w, gdn_a_log, gdn_dt_bias, gdn_norm, gdn_w_up,
           mla_q_norm, mla_w_qb, mla_kv_norm, mla_w_kvb, mla_w_up, w_out,
           norm_mem, mem_in_norm, w_mq, w_mk, w_mv, w_mo,
           norm_ffn, router_w, router_b, w_gate, b_gate, w_up, b_up, w_down, b_down, norm_final):
    depth = w_in.shape[0]
    assert depth == 1, "single-layer trunk"
    l = 0
    n = x_prompt.shape[1]
    b = x_sample.shape[0]
    past_len = page_table.shape[1] * PAGE_SIZE
    xp = x_prompt.reshape(n, D_MODEL)
    xs = x_sample.reshape(b, D_MODEL)
    row = lambda t: t.reshape(1, -1)

    wp32 = _pack_w_in(w_in[l])
    wp16 = wp32.astype(BF16)
    wq, wk, wv, wuk, wuv = _pack_mla_weights(mla_w_qb[l], mla_w_kvb[l])
    wmu32 = mla_w_up[l].reshape(MLA_HEADS, MLA_V, D_MODEL)
    wmu32 = jnp.concatenate([wmu32, jnp.zeros((MLA_HEADS, LANE - MLA_V, D_MODEL), F32)],
                            axis=1).reshape(MLA_HEADS * LANE, D_MODEL)
    rw = jnp.concatenate([router_w[l], jnp.zeros((D_MODEL, LANE - N_EXPERTS), F32)], axis=1)
    rb = _pad_lanes(router_b[l])
    alog128 = _pad_lanes(gdn_a_log[l])
    dtb128 = _pad_lanes(gdn_dt_bias[l])

    mk_p, mv_p = _memkv(mem_prompt.reshape(MEM_TOKENS, D_MODEL), row(mem_in_norm[l]),
                        w_mk[l].astype(BF16), w_mv[l].astype(BF16))
    qkv, z, ab, qa, lat, rope, sga, sgb = _inproj(xp, row(norm_mix[l]), wp16, 256)
    og_p, conv_p, ssm_p = _gdn_prompt(qkv, z, ab, gdn_conv_w[l], gdn_a_log[l], gdn_dt_bias[l], gdn_norm[l])
    tabs_p = _rope_tables(jnp.arange(n))
    q, k, v, ckv_p, kpe_p = _mla_proj(qa, lat, rope, tabs_p, row(mla_q_norm[l]), row(mla_kv_norm[l]),
                                      wq.astype(BF16), wk.astype(BF16), wv.astype(BF16), 256)
    om_p = _attn_prompt(q, k, v)
    base0 = jnp.zeros((1, LANE), F32)
    x2_p, h3_p, idx_p, wts_p, rank_p, cnt_p = _post_prompt(
        xp, og_p, om_p, sga, sgb, gdn_w_up[l].astype(BF16), wmu32.astype(BF16), w_out[l].astype(BF16),
        row(norm_mem[l]), w_mq[l].astype(BF16), mk_p.astype(BF16), mv_p.astype(BF16), w_mo[l].astype(BF16),
        row(norm_ffn[l]), rw, rb, base0)

    qkv_s, z_s, ab_s, qa_s, lat_s, rope_s, sga_s, sgb_s = _inproj(xs, row(norm_mix[l]), wp32, b, precise=True)
    conv_s, qn_s, kn_s, v_s, ege_s, be_s, zs_s = _gdn_dec_prep(
        qkv_s, state_gdn_conv.reshape(state_gdn_conv.shape[1:]), z_s, ab_s, gdn_conv_w[l], alog128, dtb128)
    per_head = lambda t: t.reshape(b * GDN_HEADS, 1, GDN_DV)
    ssm_s, og_s = _gdn_dec_step(
        state_gdn_ssm.reshape(b * GDN_HEADS, GDN_DK, GDN_DV),
        [per_head(t) for t in (qn_s, kn_s, v_s, ege_s, be_s, zs_s)], gdn_norm[l].reshape(1, 1, GDN_DV))
    og_s = og_s.reshape(b, GDN_V_W)
    tabs_s = _rope_tables(jnp.full((1,), past_len, I32))
    qlat_s, qfull_s, ckv_s, kpe_s = _mla_dec_proj(qa_s, lat_s, rope_s, tabs_s, row(mla_q_norm[l]),
                                                   row(mla_kv_norm[l]), wq, wuk)
    qpe_s = qfull_s.reshape(b, MLA_HEADS, LANE)[:, :, ROPE_LANE0:ROPE_LANE0 + MLA_ROPE]
    kpe32_s = kpe_s[:, ROPE_LANE0:ROPE_LANE0 + MLA_ROPE]
    olat_s = _paged_attn(page_table, qlat_s.reshape(b, MLA_HEADS, MLA_KV_LORA), qpe_s,
                         ckv_s.reshape(b, 1, MLA_KV_LORA), kpe32_s.reshape(b, 1, MLA_ROPE),
                         cache_kv_latent.reshape(cache_kv_latent.shape[1:]),
                         jnp.swapaxes(cache_k_rope, 2, 3).reshape(cache_k_rope.shape[1], MLA_ROPE, PAGE_SIZE))
    x1_s, qmem_s = _post_a_sample(xs, og_s, olat_s.reshape(b, MLA_HEADS * MLA_KV_LORA), sga_s, sgb_s, wuv,
                                  gdn_w_up[l], wmu32, w_out[l], row(norm_mem[l]), w_mq[l])
    omem_s = _mem_dec(qmem_s.reshape(b, 1, MEM_W), cache_mem_k.reshape(cache_mem_k.shape[1:]),
                      cache_mem_v.reshape(cache_mem_v.shape[1:]))
    x2_s, h3_s, idx_s, wts_s, rank_s, cnt = _post_b_sample(x1_s, omem_s.reshape(b, MEM_W), w_mo[l],
                                                           row(norm_ffn[l]), rw, rb, cnt_p)

    counts = cnt[0, :N_EXPERTS].astype(I32)
    offs, tile_e, tile_nv, n_tiles = _route_plan(counts, (n + b) * TOP_K)
    pos_p = (offs[idx_p[:, :TOP_K]] + rank_p[:, :TOP_K]).reshape(-1)
    pos_s = (offs[idx_s[:, :TOP_K]] + rank_s[:, :TOP_K]).reshape(-1)
    xsort = jnp.zeros((n_tiles * MOE_TE, D_MODEL), F32)
    xsort = _dispatch(pos_p, h3_p, xsort, MOE_TM)
    xsort = _dispatch(pos_s, h3_s, xsort, b)
    ysort = _ffn(tile_e, tile_nv, xsort, w_gate[l], b_gate[l], w_up[l], b_up[l], w_down[l], b_down[l], n_tiles)
    y_p = _combine(pos_p, x2_p, wts_p, row(norm_final), ysort, MOE_TM)
    y_s = _combine(pos_s, x2_s, wts_s, row(norm_final), ysort, b)

    stack = lambda t: t[None]
    return (y_p.reshape(1, n, D_MODEL), y_s.reshape(b, 1, D_MODEL),
            stack(ckv_p.reshape(1, n, MLA_KV_LORA)),
            stack(kpe_p[:, ROPE_LANE0:ROPE_LANE0 + MLA_ROPE].reshape(1, n, MLA_ROPE)),
            stack(conv_p.reshape(1, GDN_CONV - 1, GDN_CONV_W)),
            stack(ssm_p.reshape(1, GDN_HEADS, GDN_DK, GDN_DV)),
            stack(mk_p.reshape(1, MEM_TOKENS, MEM_HEADS, MEM_HEAD_DIM)),
            stack(mv_p.reshape(1, MEM_TOKENS, MEM_HEADS, MEM_HEAD_DIM)),
            stack(ckv_s.reshape(b, 1, MLA_KV_LORA)),
            stack(kpe32_s.reshape(b, 1, MLA_ROPE)),
            stack(conv_s),
            stack(ssm_s.reshape(b, GDN_HEADS, GDN_DK, GDN_DV)))
```
